```python
import jax
import jax.numpy as jnp
from jax import lax
import numpy as np

D_MODEL = 1024
BATCH = 2
SEQ = 8192
DEPTH = 4
DEC_BATCH = 128
DEC_SEQ = 4
PAST_LEN = 2048
PAGE_SIZE = 128

RW_HEADS = 8
RW_HEAD_DIM = 64
RW_DIM = RW_HEADS * RW_HEAD_DIM
RW_DECAY_RANK = 64
RW_AAA_RANK = 64
RW_GATE_RANK = 128
RW_GN_EPS = 64e-5

AT_HEADS = 8
AT_KV_HEADS = 4
AT_HEAD_DIM = 64
AT_GROUP = AT_HEADS // AT_KV_HEADS
IDX_HEADS = 8
IDX_DIM = 64
TOPK_MAX = 256
Q_BLOCK = 128
ROPE_THETA = 10000.0

N_EXPERTS = 64
N_GROUPS = 8
TOPK_GROUPS = 4
TOP_K = 8
EXPERT_FF = 256
SHARED_FF = 256
ROUTED_SCALE = 2.5
MOE_BLOCK = 64

LN_EPS = 1e-5
DN_ALPHA = (2 * DEPTH) ** 0.25
DN_BETA = (8 * DEPTH) ** -0.25

RW_SPLITS = (RW_DIM, RW_DIM, RW_DIM, RW_DECAY_RANK, RW_AAA_RANK, RW_GATE_RANK)
AT_SPLITS = (AT_HEADS * AT_HEAD_DIM, AT_KV_HEADS * AT_HEAD_DIM, AT_KV_HEADS * AT_HEAD_DIM, IDX_HEADS * IDX_DIM, IDX_DIM, IDX_HEADS)
RW_COLS = sum(RW_SPLITS)
AT_COLS = sum(AT_SPLITS)
GATE_COLS = 2 * D_MODEL
IN_COLS = RW_COLS + AT_COLS + GATE_COLS

kernel_name = 'rwkv7_dsa_moe_hybrid_step'


def split_last(p, sizes):
    return jnp.split(p, np.cumsum(sizes)[:-1].tolist(), axis=-1)


def layer_norm(x, g, b):
    xf = x.astype(jnp.float32)
    mu = xf.mean(-1, keepdims=True)
    var = jnp.mean(jnp.square(xf - mu), -1, keepdims=True)
    return ((xf - mu) * lax.rsqrt(var + LN_EPS) * g + b).astype(x.dtype)


def rope(x, pos):
    dh = x.shape[-1]
    half = dh // 2
    inv = ROPE_THETA ** (-jnp.arange(half, dtype=jnp.float32) * (2.0 / dh))
    ang = pos.astype(jnp.float32)[:, None] * inv
    cos = jnp.cos(ang)[:, None, :]
    sin = jnp.sin(ang)[:, None, :]
    x1 = x[..., :half].astype(jnp.float32)
    x2 = x[..., half:].astype(jnp.float32)
    return jnp.concatenate([x1 * cos - x2 * sin, x2 * cos + x1 * sin], -1).astype(x.dtype)


def wkv7_scan(r, w, k, v, a, b, s0):
    def step(S, inp):
        r_t, w_t, k_t, v_t, a_t, b_t = inp
        sa = jnp.einsum('bhvk,bhk->bhv', S, a_t)
        S = S * w_t[:, :, None, :] + sa[..., None] * b_t[:, :, None, :] + v_t[..., None] * k_t[:, :, None, :]
        return S, jnp.einsum('bhvk,bhk->bhv', S, r_t)
    xs = tuple(jnp.moveaxis(t, 1, 0) for t in (r, w, k, v, a, b))
    s_fin, ys = lax.scan(step, s0, xs)
    return s_fin, jnp.moveaxis(ys, 0, 1)


def rwkv7_mix(p, p_prev, s0, mu, w2, w0, a2, a0, g2, k_k, k_a, r_k, gn_g, gn_b):
    B, T, _ = p.shape
    f32 = jnp.float32
    xs = (p + mu * (p_prev - p)).astype(f32)
    r, k, v, dw, da, dg = split_last(xs, RW_SPLITS)
    w = -jax.nn.softplus(-(w0 + jnp.tanh(dw) @ w2)) - 0.5
    decay = jnp.exp(-jnp.exp(w))
    a = jax.nn.sigmoid(a0 + da @ a2)
    g = jax.nn.sigmoid(dg) @ g2
    hs = (B, T, RW_HEADS, RW_HEAD_DIM)
    kk = (k * k_k).reshape(hs)
    kk = kk / jnp.maximum(jnp.sqrt(jnp.sum(kk * kk, -1, keepdims=True)), 1e-12)
    k = k * (1.0 + (a - 1.0) * k_a)
    r, k, v, decay, a = (t.reshape(hs) for t in (r, k, v, decay, a))
    s_fin, y = wkv7_scan(r, decay, k, v, -kk, kk * a, s0.astype(f32))
    ym = y.mean(-1, keepdims=True)
    yv = jnp.mean(jnp.square(y - ym), -1, keepdims=True)
    y = ((y - ym) * lax.rsqrt(yv + RW_GN_EPS)).reshape(B, T, RW_DIM) * gn_g + gn_b
    bonus = (jnp.sum(r * k * r_k, -1, keepdims=True) * v).reshape(B, T, RW_DIM)
    return (y + bonus) * g, s_fin


def indexer_scores(qi, wi, ki):
    dots = jnp.einsum('bthd,bld->bthl', qi, ki).astype(jnp.float32) * IDX_DIM ** -0.5
    return jnp.einsum('bth,bthl->btl', wi.astype(jnp.float32) * IDX_HEADS ** -0.5, jax.nn.relu(dots))


def sparse_attend(q, kg, vg, valid):
    B, T = q.shape[:2]
    qg = q.reshape(B, T, AT_KV_HEADS, AT_GROUP, AT_HEAD_DIM)
    s = jnp.einsum('btkgd,btskd->btkgs', qg, kg).astype(jnp.float32) * AT_HEAD_DIM ** -0.5
    s = jnp.where(valid[:, :, None, None, :], s, -jnp.inf)
    pr = jax.nn.softmax(s, axis=-1).astype(vg.dtype)
    o = jnp.einsum('btkgs,btskd->btkgd', pr, vg)
    return o.reshape(B, T, AT_HEADS * AT_HEAD_DIM)


def dsa_prompt(q, k, v, qi, wi, ki):
    B, S = q.shape[:2]
    topk = min(TOPK_MAX, S // 4)
    nb = S // Q_BLOCK
    key_pos = jnp.arange(S)

    def to_blocks(t):
        return t.reshape((B, nb, Q_BLOCK) + t.shape[2:]).swapaxes(0, 1)

    def block(args):
        qb, qib, wib, t0 = args
        tpos = t0 + jnp.arange(Q_BLOCK)
        sc = indexer_scores(qib, wib, ki)
        sc = jnp.where((key_pos[None, :] <= tpos[:, None])[None], sc, -jnp.inf)
        _, sel = lax.top_k(sc, topk)
        valid = sel <= tpos[None, :, None]
        kg = jax.vmap(lambda kb, sb: kb[sb])(k, sel)
        vg = jax.vmap(lambda vb, sb: vb[sb])(v, sel)
        return sparse_attend(qb, kg, vg, valid)

    out = lax.map(block, (to_blocks(q), to_blocks(qi), to_blocks(wi), jnp.arange(nb) * Q_BLOCK))
    return out.swapaxes(0, 1).reshape(B, S, -1)


def dsa_sample(q, k_new, v_new, qi, wi, ki_new, pool_k, pool_v, pool_idx, page_table):
    DB, T = q.shape[:2]
    past = page_table.shape[1] * PAGE_SIZE
    L = past + T
    topk = min(TOPK_MAX, L // 4)
    ki_past = pool_idx[page_table].reshape(DB, past, IDX_DIM)
    ki_all = jnp.concatenate([ki_past, ki_new], axis=1)
    tpos = past + jnp.arange(T)
    sc = indexer_scores(qi, wi, ki_all)
    sc = jnp.where((jnp.arange(L)[None, :] <= tpos[:, None])[None], sc, -jnp.inf)
    _, sel = lax.top_k(sc, topk)
    valid = sel <= tpos[None, :, None]
    in_past = (sel < past)[..., None, None]
    s_past = jnp.minimum(sel, past - 1)
    rows = page_table[jnp.arange(DB)[:, None, None], s_past // PAGE_SIZE] * PAGE_SIZE + s_past % PAGE_SIZE
    s_new = jnp.clip(sel - past, 0, T - 1)

    def gather(pool, new):
        from_pool = pool.reshape((-1,) + pool.shape[2:])[rows]
        from_new = jax.vmap(lambda nb_, sb: nb_[sb])(new, s_new)
        return jnp.where(in_past, from_pool, from_new)

    return sparse_attend(q, gather(pool_k, k_new), gather(pool_v, v_new), valid)


def token_mixer(x, pos, shift_row, s0, attend, w_in, b_gate, rw_mu, rw_w2, rw_w0, rw_a2, rw_a0, rw_g2,
                rw_kk, rw_ka, rw_rk, rw_gn_g, rw_gn_b, rw_proj, at_proj, w_out):
    B, T, _ = x.shape
    p = x @ w_in
    p_rw = p[..., :RW_COLS]
    p_at = p[..., RW_COLS:RW_COLS + AT_COLS]
    p_gate = p[..., RW_COLS + AT_COLS:]
    p0 = shift_row @ w_in[:, :RW_COLS]
    p_rw_prev = jnp.concatenate([p0[:, None], p_rw[:, :-1]], axis=1)
    o_rw, s_fin = rwkv7_mix(p_rw, p_rw_prev, s0, rw_mu, rw_w2, rw_w0, rw_a2, rw_a0, rw_g2,
                            rw_kk, rw_ka, rw_rk, rw_gn_g, rw_gn_b)
    q, k, v, qi, ki, wi = split_last(p_at, AT_SPLITS)
    q = rope(q.reshape(B, T, AT_HEADS, AT_HEAD_DIM), pos)
    k = rope(k.reshape(B, T, AT_KV_HEADS, AT_HEAD_DIM), pos)
    v = v.reshape(B, T, AT_KV_HEADS, AT_HEAD_DIM)
    qi = rope(qi.reshape(B, T, IDX_HEADS, IDX_DIM), pos)
    ki = rope(ki.reshape(B, T, 1, IDX_DIM), pos)[:, :, 0]
    o_at = attend(q, k, v, qi, wi, ki)
    g_rw, g_at = jnp.split(jax.nn.sigmoid(p_gate + b_gate), 2, axis=-1)
    merged = g_rw * (o_rw.astype(x.dtype) @ rw_proj) + g_at * (o_at @ at_proj)
    return merged @ w_out, k, v, ki, s_fin, x[:, -1]


def swiglu(x, w1, w3, w2):
    return (jax.nn.silu(x @ w1) * (x @ w3)) @ w2


def grouped_experts(x2, eidx, gw, w1, w3, w2):
    T, D = x2.shape
    A = T * TOP_K
    e_flat = eidx.reshape(A)
    order = jnp.argsort(e_flat)
    e_sorted = e_flat[order]
    tok_sorted = order // TOP_K
    gw_sorted = gw.reshape(A)[order]
    counts = jnp.bincount(e_flat, length=N_EXPERTS)
    padded = (counts + MOE_BLOCK - 1) // MOE_BLOCK * MOE_BLOCK
    pad_end = jnp.cumsum(padded)
    pad_start = pad_end - padded
    start = jnp.cumsum(counts) - counts
    dest = pad_start[e_sorted] + jnp.arange(A) - start[e_sorted]
    n_blocks = -(-A // MOE_BLOCK) + N_EXPERTS
    slot_tok = jnp.full((n_blocks * MOE_BLOCK,), T, jnp.int32).at[dest].set(tok_sorted.astype(jnp.int32))
    block_exp = jnp.minimum(jnp.searchsorted(pad_end, jnp.arange(n_blocks) * MOE_BLOCK, side='right'), N_EXPERTS - 1)
    xpad = jnp.concatenate([x2, jnp.zeros((1, D), x2.dtype)], 0)

    def run(args):
        toks, e = args
        return swiglu(xpad[toks], w1[e], w3[e], w2[e])

    out = lax.map(run, (slot_tok.reshape(n_blocks, MOE_BLOCK), block_exp))
    y_assign = out.reshape(-1, D)[dest] * gw_sorted[:, None]
    return jax.ops.segment_sum(y_assign, tok_sorted, num_segments=T)


def moe_ffn(x2, router_w, router_b, w1, w3, w2, ws1, ws3, ws2):
    T = x2.shape[0]
    s = jax.nn.sigmoid((x2 @ router_w).astype(jnp.float32))
    sb = s + router_b.astype(jnp.float32)
    gscore = lax.top_k(sb.reshape(T, N_GROUPS, N_EXPERTS // N_GROUPS), 2)[0].sum(-1)
    _, gsel = lax.top_k(gscore, TOPK_GROUPS)
    gmask = jax.nn.one_hot(gsel, N_GROUPS, dtype=jnp.float32).sum(1) > 0
    emask = jnp.repeat(gmask, N_EXPERTS // N_GROUPS, axis=1)
    _, eidx = lax.top_k(jnp.where(emask, sb, -jnp.inf), TOP_K)
    gw = jnp.take_along_axis(s, eidx, axis=1)
    gw = gw / gw.sum(-1, keepdims=True) * ROUTED_SCALE
    return grouped_experts(x2, eidx, gw.astype(x2.dtype), w1, w3, w2) + swiglu(x2, ws1, ws3, ws2)


def setup_inputs(seed: int = 0) -> dict:
    key = jax.random.key(seed)
    keys = iter(jax.random.split(key, 48))

    def nrm(shape, scale):
        return jax.random.normal(next(keys), shape, jnp.float32) * scale

    def unif(shape, lo, hi):
        return jax.random.uniform(next(keys), shape, jnp.float32, lo, hi)

    n_pages = PAST_LEN // PAGE_SIZE
    n_used = DEC_BATCH * n_pages
    n_pool = n_used + max(1, n_used // 4)
    page_table = jax.random.permutation(next(keys), n_pool)[:n_used].reshape(DEC_BATCH, n_pages).astype(jnp.int32)
    L = DEPTH
    d = D_MODEL
    return {
        'x_prompt': nrm((BATCH, SEQ, d), 1.0),
        'x_sample': nrm((DEC_BATCH, DEC_SEQ, d), 1.0),
        'cache_k': nrm((L, n_pool, PAGE_SIZE, AT_KV_HEADS, AT_HEAD_DIM), 1.0),
        'cache_v': nrm((L, n_pool, PAGE_SIZE, AT_KV_HEADS, AT_HEAD_DIM), 1.0),
        'cache_idx': nrm((L, n_pool, PAGE_SIZE, IDX_DIM), 1.0),
        'state_rwkv': nrm((L, DEC_BATCH, RW_HEADS, RW_HEAD_DIM, RW_HEAD_DIM), 0.3),
        'state_shift': nrm((L, DEC_BATCH, d), 1.0),
        'page_table': page_table,
        'w_in': nrm((L, d, IN_COLS), d ** -0.5),
        'b_gate': nrm((L, GATE_COLS), 0.1),
        'rw_mu': unif((L, RW_COLS), 0.0, 1.0),
        'rw_w2': nrm((L, RW_DECAY_RANK, RW_DIM), 0.5 * RW_DECAY_RANK ** -0.5),
        'rw_w0': unif((L, RW_DIM), -6.0, 1.0),
        'rw_a2': nrm((L, RW_AAA_RANK, RW_DIM), 0.5 * RW_AAA_RANK ** -0.5),
        'rw_a0': nrm((L, RW_DIM), 0.1),
        'rw_g2': nrm((L, RW_GATE_RANK, RW_DIM), RW_GATE_RANK ** -0.5),
        'rw_kk': 0.85 + nrm((L, RW_DIM), 0.05),
        'rw_ka': 1.0 + nrm((L, RW_DIM), 0.05),
        'rw_rk': nrm((L, RW_HEADS, RW_HEAD_DIM), 0.1),
        'rw_gn_g': 1.0 + nrm((L, RW_DIM), 0.02),
        'rw_gn_b': nrm((L, RW_DIM), 0.02),
        'rw_proj': nrm((L, RW_DIM, d), RW_DIM ** -0.5),
        'at_proj': nrm((L, AT_HEADS * AT_HEAD_DIM, d), (AT_HEADS * AT_HEAD_DIM) ** -0.5),
        'w_out': nrm((L, d, d), DN_BETA * d ** -0.5),
        'ln1_g': 1.0 + nrm((L, d), 0.01),
        'ln1_b': nrm((L, d), 0.01),
        'router_w': nrm((L, d, N_EXPERTS), d ** -0.5),
        'router_b': nrm((L, N_EXPERTS), 0.01),
        'ex_w1': nrm((L, N_EXPERTS, d, EXPERT_FF), d ** -0.5),
        'ex_w3': nrm((L, N_EXPERTS, d, EXPERT_FF), d ** -0.5),
        'ex_w2': nrm((L, N_EXPERTS, EXPERT_FF, d), DN_BETA * EXPERT_FF ** -0.5),
        'sh_w1': nrm((L, d, SHARED_FF), d ** -0.5),
        'sh_w3': nrm((L, d, SHARED_FF), d ** -0.5),
        'sh_w2': nrm((L, SHARED_FF, d), DN_BETA * SHARED_FF ** -0.5),
        'ln2_g': 1.0 + nrm((L, d), 0.01),
        'ln2_b': nrm((L, d), 0.01),
    }


def reference(x_prompt, x_sample, cache_k, cache_v, cache_idx, state_rwkv, state_shift, page_table,
              w_in, b_gate, rw_mu, rw_w2, rw_w0, rw_a2, rw_a0, rw_g2, rw_kk, rw_ka, rw_rk, rw_gn_g, rw_gn_b,
              rw_proj, at_proj, w_out, ln1_g, ln1_b, router_w, router_b, ex_w1, ex_w3, ex_w2,
              sh_w1, sh_w3, sh_w2, ln2_g, ln2_b):
    bp, sp, _ = x_prompt.shape
    ts = x_sample.shape[1]
    past = page_table.shape[1] * PAGE_SIZE
    pos_p = jnp.arange(sp)
    pos_s = past + jnp.arange(ts)
    shift_p0 = jnp.zeros((bp, D_MODEL), x_prompt.dtype)
    rwkv_p0 = jnp.zeros((bp, RW_HEADS, RW_HEAD_DIM, RW_HEAD_DIM), jnp.float32)
    xp, xs = x_prompt, x_sample
    new_p = ([], [], [], [], [])
    new_s = ([], [], [], [], [])
    for l in range(DEPTH):
        mix_w = (w_in[l], b_gate[l], rw_mu[l], rw_w2[l], rw_w0[l], rw_a2[l], rw_a0[l], rw_g2[l],
                 rw_kk[l], rw_ka[l], rw_rk[l], rw_gn_g[l], rw_gn_b[l], rw_proj[l], at_proj[l], w_out[l])
        moe_w = (router_w[l], router_b[l], ex_w1[l], ex_w3[l], ex_w2[l], sh_w1[l], sh_w3[l], sh_w2[l])
        attend_s = lambda q, k, v, qi, wi, ki, l=l: dsa_sample(q, k, v, qi, wi, ki, cache_k[l], cache_v[l],
                                                              cache_idx[l], page_table)
        m_p, *st_p = token_mixer(xp, pos_p, shift_p0, rwkv_p0, dsa_prompt, *mix_w)
        m_s, *st_s = token_mixer(xs, pos_s, state_shift[l], state_rwkv[l], attend_s, *mix_w)
        for acc, val in zip(new_p, st_p):
            acc.append(val)
        for acc, val in zip(new_s, st_s):
            acc.append(val)
        xp = layer_norm(DN_ALPHA * xp + m_p, ln1_g[l], ln1_b[l])
        xs = layer_norm(DN_ALPHA * xs + m_s, ln1_g[l], ln1_b[l])
        xp = layer_norm(DN_ALPHA * xp + moe_ffn(xp.reshape(-1, D_MODEL), *moe_w).reshape(xp.shape), ln2_g[l], ln2_b[l])
        xs = layer_norm(DN_ALPHA * xs + moe_ffn(xs.reshape(-1, D_MODEL), *moe_w).reshape(xs.shape), ln2_g[l], ln2_b[l])
    k_p, v_p, idx_p, rwkv_p, shift_p = (jnp.stack(a) for a in new_p)
    k_s, v_s, idx_s, rwkv_s, shift_s = (jnp.stack(a) for a in new_s)
    return (xp, xs, k_p, v_p, idx_p, rwkv_p, shift_p, k_s, v_s, idx_s, rwkv_s, shift_s)
```

```python
import functools

import numpy as np
import jax
import jax.numpy as jnp
from jax import lax
from jax.experimental import pallas as pl
from jax.experimental.pallas import tpu as pltpu

F32 = jnp.float32
BF16 = jnp.bfloat16
I32 = jnp.int32
HIGHEST = lax.Precision.HIGHEST

RW_HEADS = 8
RW_HEAD_DIM = 64
RW_DIM = RW_HEADS * RW_HEAD_DIM
RW_DECAY_RANK = 64
RW_AAA_RANK = 64
RW_GATE_RANK = 128
RW_GN_EPS = 64e-5
RW_COLS = 3 * RW_DIM + RW_DECAY_RANK + RW_AAA_RANK + RW_GATE_RANK

AT_HEADS = 8
AT_KV_HEADS = 4
AT_HEAD_DIM = 64
AT_GROUP = AT_HEADS // AT_KV_HEADS
AT_Q = AT_HEADS * AT_HEAD_DIM
AT_KV = AT_KV_HEADS * AT_HEAD_DIM
IDX_HEADS = 8
IDX_DIM = 64
TOPK_MAX = 256
PAGE_SIZE = 128
ROPE_THETA = 10000.0

N_EXPERTS = 64
N_GROUPS = 8
GROUP_SIZE = N_EXPERTS // N_GROUPS
TOPK_GROUPS = 4
TOP_K = 8
EXPERT_FF = 256
SHARED_FF = 256
ROUTED_SCALE = 2.5
LN_EPS = 1e-5

LANES = 128
SUBLANES = 8
VMEM_LIMIT = 56 * 1024 * 1024

C_RW = 0
C_Q = C_RW + RW_COLS
C_K = C_Q + AT_Q
C_V = C_K + AT_KV
C_QI = C_V + AT_KV
C_KW = C_QI + IDX_HEADS * IDX_DIM
C_GATE = C_KW + LANES
INT_MIN = np.int32(-2 ** 31)


def _dot(a, b, **kw):
    return jnp.dot(a, b, preferred_element_type=F32, **kw)


def _dot_nt(a, b):
    return lax.dot_general(a, b, (((1,), (1,)), ((), ())), preferred_element_type=F32)


def _dot_tn(a, b):
    return lax.dot_general(a, b, (((0,), (0,)), ((), ())), preferred_element_type=F32)


def _params(*sem):
    return pltpu.CompilerParams(dimension_semantics=sem, vmem_limit_bytes=VMEM_LIMIT)


def _sigmoid(x):
    return 1.0 / (1.0 + jnp.exp(-x))


def _sort_key(s):
    s = jnp.where(s == 0.0, 0.0, s)
    bits = lax.bitcast_convert_type(s, I32)
    return bits ^ ((bits >> 31) & np.int32(0x7FFFFFFF))


def _rope_table_kernel(inv_ref, cos_ref, sin_ref, *, tm, seq, past, dec_seq):
    i = pl.program_id(0)
    row = i * tm + lax.broadcasted_iota(I32, (tm, LANES), 0)
    pos = jnp.where(row < seq, row, past + lax.rem(jnp.maximum(row - seq, 0), dec_seq))
    ang = pos.astype(F32) * inv_ref[...]
    lane = lax.broadcasted_iota(I32, (tm, LANES), 1)
    s = jnp.sin(ang)
    cos_ref[...] = jnp.cos(ang)
    sin_ref[...] = jnp.where((lane & (AT_HEAD_DIM - 1)) < AT_HEAD_DIM // 2, -s, s)


def rope_table(seq, past, dec_seq, tm):
    half = AT_HEAD_DIM // 2
    inv = (np.float32(ROPE_THETA) ** (-np.arange(half, dtype=np.float32) * np.float32(2.0 / AT_HEAD_DIM))).astype(np.float32)
    inv_row = jnp.asarray(np.tile(inv, LANES // half)[None, :])
    rows = seq + tm
    out = jax.ShapeDtypeStruct((rows, LANES), F32)
    return pl.pallas_call(
        functools.partial(_rope_table_kernel, tm=tm, seq=seq, past=past, dec_seq=dec_seq),
        grid=(rows // tm,),
        in_specs=[pl.BlockSpec((1, LANES), lambda i: (0, 0))],
        out_specs=[pl.BlockSpec((tm, LANES), lambda i: (i, 0))] * 2,
        out_shape=[out, out],
        compiler_params=_params("parallel"),
        name="rope_table",
    )(inv_row)


def _rope(x, cos, sin_signed, lo):
    up = pltpu.roll(x, LANES - AT_HEAD_DIM // 2, axis=1)
    dn = pltpu.roll(x, AT_HEAD_DIM // 2, axis=1)
    return x * cos + jnp.where(lo, up, dn) * sin_signed


def _proj_kernel(x_ref, w_ref, bg_ref, cos_ref, sin_ref,
                 prw_ref, q_ref, k_ref, v_ref, qi_ref, kw_ref, gate_ref, kbf_ref, vbf_ref, kwbf_ref):
    xb = x_ref[...].astype(BF16)
    tm = xb.shape[0]
    cos = cos_ref[...]
    sin = sin_ref[...]
    lane = lax.broadcasted_iota(I32, (tm, LANES), 1)
    lo = (lane & (AT_HEAD_DIM - 1)) < AT_HEAD_DIM // 2

    prw_ref[...] = _dot(xb, w_ref[:, C_RW:C_Q])

    def roped(c0, width, out_ref, bf_ref=None):
        p = _dot(xb, w_ref[:, c0:c0 + width])
        for g in range(width // LANES):
            sl = slice(g * LANES, (g + 1) * LANES)
            y = _rope(p[:, sl], cos, sin, lo)
            out_ref[:, sl] = y
            if bf_ref is not None:
                bf_ref[:, sl] = y.astype(BF16)

    roped(C_Q, AT_Q, q_ref)
    roped(C_K, AT_KV, k_ref, kbf_ref)
    roped(C_QI, IDX_HEADS * IDX_DIM, qi_ref)
    v = _dot(xb, w_ref[:, C_V:C_QI])
    v_ref[...] = v
    vbf_ref[...] = v.astype(BF16)
    kw = _dot(xb, w_ref[:, C_KW:C_GATE])
    kw = jnp.where(lane < IDX_DIM, _rope(kw, cos, sin, lo), kw)
    kw_ref[...] = kw
    kwbf_ref[...] = kw.astype(BF16)
    gate_ref[...] = _sigmoid(_dot(xb, w_ref[:, C_GATE:]) + bg_ref[...])


def project(x, w_all, b_gate, cos_t, sin_t, *, tm, n_prompt, seq):
    n, d = x.shape
    n_tab = seq // tm
    npt = n_prompt // tm

    def tab_map(i):
        return (jnp.where(i < npt, lax.rem(i, n_tab), n_tab), 0)

    row = lambda w: pl.BlockSpec((tm, w), lambda i: (i, 0))
    outs = [(RW_COLS, F32), (AT_Q, F32), (AT_KV, F32), (AT_KV, F32), (IDX_HEADS * IDX_DIM, F32), (LANES, F32),
            (2 * d, F32), (AT_KV, BF16), (AT_KV, BF16), (LANES, BF16)]
    return pl.pallas_call(
        _proj_kernel,
        grid=(n // tm,),
        in_specs=[row(d),
                  pl.BlockSpec(w_all.shape, lambda i: (0, 0)),
                  pl.BlockSpec((1, 2 * d), lambda i: (0, 0)),
                  pl.BlockSpec((tm, LANES), tab_map),
                  pl.BlockSpec((tm, LANES), tab_map)],
        out_specs=[row(w) for w, _ in outs],
        out_shape=[jax.ShapeDtypeStruct((n, w), dt) for w, dt in outs],
        compiler_params=_params("parallel"),
        name="in_proj",
    )(x, w_all, b_gate, cos_t, sin_t)


def _mm_kernel(x_ref, w_ref, o_ref):
    o_ref[...] = _dot(x_ref[...].astype(BF16), w_ref[...])


def shift_project(rows, w_all):
    m, d = rows.shape
    return pl.pallas_call(
        _mm_kernel,
        grid=(1,),
        in_specs=[pl.BlockSpec((m, d), lambda i: (0, 0)), pl.BlockSpec((d, RW_COLS), lambda i: (0, 0))],
        out_specs=pl.BlockSpec((m, RW_COLS), lambda i: (0, 0)),
        out_shape=jax.ShapeDtypeStruct((m, RW_COLS), F32),
        compiler_params=_params("arbitrary"),
        name="shift_proj",
    )(rows, w_all)


def _wkv_kernel(p_ref, pp_ref, s0_ref, mu_ref, wwa_ref, g2_ref, vec_ref, o_ref, sfin_ref, s_scr, *, C, valid):
    j = pl.program_id(1)
    D = RW_DIM
    N = RW_HEAD_DIM

    @pl.when(j == 0)
    def _():
        s_scr[...] = s0_ref[0]

    p = p_ref[0]
    xs = p + mu_ref[...] * (pp_ref[0] - p)
    r = xs[:, 0:D]
    k = xs[:, D:2 * D]
    v = xs[:, 2 * D:3 * D]
    x_lr = xs[:, 3 * D:3 * D + LANES]
    lane = lax.broadcasted_iota(I32, (C, LANES), 1)
    lr_in = jnp.where(lane < RW_DECAY_RANK, jnp.tanh(x_lr), x_lr)
    lr = _dot(lr_in.astype(BF16), wwa_ref[...])
    w0, a0, kkw, ka, rk, gng, gnb = (vec_ref[i:i + 1, :] for i in range(7))
    z = -(w0 + lr[:, :D])
    softplus = jnp.maximum(z, 0.0) + jnp.log1p(jnp.exp(-jnp.abs(z)))
    logw = -jnp.exp(-softplus - 0.5)
    if valid < C:
        rowid = lax.broadcasted_iota(I32, (C, D), 0)
        logw = jnp.where(rowid < valid, logw, 0.0)
    a_sig = _sigmoid(a0 + lr[:, D:])
    g = _dot(_sigmoid(xs[:, 3 * D + LANES:]).astype(BF16), g2_ref[...])
    kkv = k * kkw
    k2 = k * (1.0 + (a_sig - 1.0) * ka)
    bonus_in = r * k2 * rk

    ri = lax.broadcasted_iota(I32, (C, C), 0)
    ci = lax.broadcasted_iota(I32, (C, C), 1)
    strict = ri > ci
    incl = ri >= ci
    cl = _dot(incl.astype(F32), logw, precision=HIGHEST)
    e_pos = jnp.exp(cl)
    e_neg = jnp.exp(-cl)
    e_prev = jnp.exp(cl - logw)
    rt = r * e_pos
    kt = k2 * e_neg
    p_end = e_pos[C - 1:C, :]

    for h in range(RW_HEADS):
        sl = slice(h * N, (h + 1) * N)
        kk = kkv[:, sl]
        kk = kk / jnp.maximum(jnp.sqrt(jnp.sum(kk * kk, axis=-1, keepdims=True)), 1e-12)
        at = -kk * e_prev[:, sl]
        bt = kk * a_sig[:, sl] * e_neg[:, sl]
        ar = jnp.concatenate([at, rt[:, sl]], axis=0).astype(BF16)
        bk = jnp.concatenate([bt, kt[:, sl]], axis=0).astype(BF16)
        pm = _dot_nt(ar, bk)
        a_ab = jnp.where(strict, pm[:C, :C], 0.0)
        a_ak = jnp.where(strict, pm[:C, C:], 0.0)
        a_rb = jnp.where(incl, pm[C:, :C], 0.0)
        a_rk = jnp.where(incl, pm[C:, C:], 0.0)
        nn = a_ab
        lp = a_ab
        for _ in range(max(C.bit_length() - 2, 0)):
            lpb = lp.astype(BF16)
            lp = _dot(lpb, lpb)
            nn = nn + lp + _dot(nn.astype(BF16), lp.astype(BF16))
        s0 = s_scr[h]
        vh = v[:, sl]
        x0 = _dot_nt(ar, s0.astype(BF16))
        av = _dot(jnp.concatenate([a_ak, a_rk], axis=0).astype(BF16), vh.astype(BF16))
        rhs = x0[:C] + av[:C]
        u = rhs + _dot(nn.astype(BF16), rhs.astype(BF16))
        y = x0[C:] + av[C:] + _dot(a_rb.astype(BF16), u.astype(BF16))
        uv = jnp.concatenate([u, vh], axis=0).astype(BF16)
        s_scr[h] = (s0 + _dot_tn(uv, bk)) * p_end[:, sl]

        ym = jnp.mean(y, axis=-1, keepdims=True)
        yc = y - ym
        yv = jnp.mean(yc * yc, axis=-1, keepdims=True)
        yn = yc * lax.rsqrt(yv + RW_GN_EPS) * gng[:, sl] + gnb[:, sl]
        bonus = jnp.sum(bonus_in[:, sl], axis=-1, keepdims=True) * vh
        o_ref[0, :, sl] = (yn + bonus) * g[:, sl]

    @pl.when(j == pl.num_programs(1) - 1)
    def _():
        sfin_ref[0] = s_scr[...]


def rwkv7(p, p_prev, s0, mu, wwa, g2, vecs, *, C, valid):
    nseq, T, _ = p.shape
    tok = pl.BlockSpec((1, C, RW_COLS), lambda b, j: (b, j, 0))
    st = pl.BlockSpec((1, RW_HEADS, RW_HEAD_DIM, RW_HEAD_DIM), lambda b, j: (b, 0, 0, 0))
    full = lambda a: pl.BlockSpec(a.shape, lambda b, j: (0,) * a.ndim)
    return pl.pallas_call(
        functools.partial(_wkv_kernel, C=C, valid=valid),
        grid=(nseq, T // C),
        in_specs=[tok, tok, st, full(mu), full(wwa), full(g2), full(vecs)],
        out_specs=[pl.BlockSpec((1, C, RW_DIM), lambda b, j: (b, j, 0)), st],
        out_shape=[jax.ShapeDtypeStruct((nseq, T, RW_DIM), F32), jax.ShapeDtypeStruct(s0.shape, F32)],
        scratch_shapes=[pltpu.VMEM((RW_HEADS, RW_HEAD_DIM, RW_HEAD_DIM), F32)],
        compiler_params=_params("parallel", "arbitrary"),
        name="rwkv7_chunk%d" % C,
    )(p, p_prev, s0, mu, wwa, g2, vecs)


def _kth_largest_key(count_ge, rows, topk):
    def body(i, t):
        cand = t + jnp.left_shift(np.int32(1), 31 - i)
        return jnp.where(count_ge(cand) >= topk, cand, t)
    return lax.fori_loop(0, 32, body, jnp.full((rows, 1), INT_MIN, I32))


def _tie_limit(count_tie_below, need, rows, ncols):
    nbits = max((ncols - 1).bit_length(), 1)

    def body(i, x):
        cand = x + jnp.left_shift(np.int32(1), nbits - 1 - i)
        return jnp.where(count_tie_below(cand) < need, cand, x)
    return lax.fori_loop(0, nbits, body, jnp.zeros((rows, 1), I32))


def _dsa_prompt_kernel(q_ref, qi_ref, kwq_ref, kbf_ref, vbf_ref, kibf_ref, o_ref,
                       key_scr, jl_scr, m_scr, l_scr, acc_scr, *, QB, topk, S):
    j = pl.program_id(1)
    nkb = j + 1
    rowpos = j * QB + lax.broadcasted_iota(I32, (QB, 1), 0)
    col0 = lax.broadcasted_iota(I32, (QB, QB), 1)

    def cols(kb):
        return pl.ds(pl.multiple_of(kb * QB, QB), QB)

    qi = qi_ref[0] * np.float32(IDX_DIM ** -0.5)
    qih = [qi[:, h * IDX_DIM:(h + 1) * IDX_DIM].astype(BF16) for h in range(IDX_HEADS)]
    kwq = kwq_ref[0]
    wih = [kwq[:, IDX_DIM + h:IDX_DIM + h + 1] * np.float32(IDX_HEADS ** -0.5) for h in range(IDX_HEADS)]

    def score_block(kb, _):
        ki = kibf_ref[0, cols(kb), :][:, :IDX_DIM]
        sc = jnp.zeros((QB, QB), F32)
        for h in range(IDX_HEADS):
            sc = sc + jnp.maximum(_dot_nt(qih[h], ki), 0.0) * wih[h]
        key = jnp.where(kb * QB + col0 <= rowpos, _sort_key(sc), INT_MIN)
        key_scr[:, cols(kb)] = key
        return 0
    lax.fori_loop(0, nkb, score_block, 0)

    def count(pred):
        def blk(kb, acc):
            kblk = key_scr[:, cols(kb)]
            for c in range(QB // LANES):
                acc = acc + jnp.where(pred(kblk[:, c * LANES:(c + 1) * LANES], kb * QB + c * LANES), 1.0, 0.0)
            return acc
        acc = lax.fori_loop(0, nkb, blk, jnp.zeros((QB, LANES), F32))
        return jnp.sum(acc, axis=1, keepdims=True).astype(I32)

    thr = _kth_largest_key(lambda cand: count(lambda kv, c0: kv >= cand), QB, topk)
    n_gt = count(lambda kv, c0: kv > thr)
    n_ge = count(lambda kv, c0: kv >= thr)
    excess = (n_ge > topk) & (thr > INT_MIN)
    jl_scr[...] = jnp.full((QB, 1), S, I32)

    @pl.when(jnp.max(jnp.where(excess, 1.0, 0.0)) > 0.5)
    def _():
        lane0 = lax.broadcasted_iota(I32, (QB, LANES), 1)
        lim = _tie_limit(lambda cand: count(lambda kv, c0: (kv == thr) & (c0 + lane0 < cand)), topk - n_gt, QB, S)
        jl_scr[...] = jnp.where(excess, lim, S)

    jlim = jl_scr[...]

    q = q_ref[0] * np.float32(AT_HEAD_DIM ** -0.5)
    qh = [q[:, h * AT_HEAD_DIM:(h + 1) * AT_HEAD_DIM].astype(BF16) for h in range(AT_HEADS)]
    m_scr[...] = jnp.full(m_scr.shape, -jnp.inf, F32)
    l_scr[...] = jnp.zeros(l_scr.shape, F32)
    acc_scr[...] = jnp.zeros(acc_scr.shape, F32)

    def attend_block(kb, _):
        kblk = key_scr[:, cols(kb)]
        colpos = kb * QB + col0
        tie_bias = jnp.where(kblk == thr, jnp.where(colpos <= jlim, 0.0, -jnp.inf), -jnp.inf)
        bias = jnp.where(colpos <= rowpos, jnp.where(kblk > thr, 0.0, tie_bias), -jnp.inf)
        kk = kbf_ref[0, cols(kb), :]
        vv = vbf_ref[0, cols(kb), :]
        for h in range(AT_HEADS):
            kvs = slice((h // AT_GROUP) * AT_HEAD_DIM, (h // AT_GROUP + 1) * AT_HEAD_DIM)
            s = _dot_nt(qh[h], kk[:, kvs]) + bias
            m_old = m_scr[h]
            m_new = jnp.maximum(m_old, jnp.max(s, axis=-1, keepdims=True))
            m_use = jnp.where(m_new == -jnp.inf, 0.0, m_new)
            pr = jnp.exp(s - m_use)
            alpha = jnp.exp(m_old - m_use)
            l_scr[h] = alpha * l_scr[h] + jnp.sum(pr, axis=-1, keepdims=True)
            hs = slice(h * AT_HEAD_DIM, (h + 1) * AT_HEAD_DIM)
            acc_scr[:, hs] = alpha * acc_scr[:, hs] + _dot(pr.astype(BF16), vv[:, kvs])
            m_scr[h] = m_new
        return 0
    lax.fori_loop(0, nkb, attend_block, 0)

    for h in range(AT_HEADS):
        hs = slice(h * AT_HEAD_DIM, (h + 1) * AT_HEAD_DIM)
        o_ref[0, :, hs] = acc_scr[:, hs] / l_scr[h]


def dsa_prompt(q, qi, kw, k_bf, v_bf, kw_bf, *, QB):
    B, S, _ = q.shape
    topk = min(TOPK_MAX, S // 4)
    qblk = lambda w: pl.BlockSpec((1, QB, w), lambda b, j: (b, j, 0))
    seq = lambda w: pl.BlockSpec((1, S, w), lambda b, j: (b, 0, 0))
    return pl.pallas_call(
        functools.partial(_dsa_prompt_kernel, QB=QB, topk=topk, S=S),
        grid=(B, S // QB),
        in_specs=[qblk(AT_Q), qblk(IDX_HEADS * IDX_DIM), qblk(LANES), seq(AT_KV), seq(AT_KV), seq(LANES)],
        out_specs=qblk(AT_Q),
        out_shape=jax.ShapeDtypeStruct((B, S, AT_Q), F32),
        scratch_shapes=[pltpu.VMEM((QB, S), I32), pltpu.VMEM((QB, 1), I32),
                        pltpu.VMEM((AT_HEADS, QB, 1), F32), pltpu.VMEM((AT_HEADS, QB, 1), F32),
                        pltpu.VMEM((QB, AT_Q), F32)],
        compiler_params=_params("parallel", "arbitrary"),
        name="dsa_prompt",
    )(q, qi, kw, k_bf, v_bf, kw_bf)


def _dsa_sample_kernel(pt_ref, l_ref, qi_ref, wi_ref, qbd_ref, kn_ref, vn_ref, kin_ref, *rest, n_pages, T, topk):
    kp = rest[:n_pages]
    vp = rest[n_pages:2 * n_pages]
    ip = rest[2 * n_pages:3 * n_pages]
    o_ref, kcat, kc, vc = rest[3 * n_pages:]
    past = n_pages * PAGE_SIZE
    LP = past + PAGE_SIZE
    TP = kn_ref.shape[1]
    for jj in range(n_pages):
        rows = slice(jj * PAGE_SIZE, (jj + 1) * PAGE_SIZE)
        kcat[rows, :] = ip[jj][...]
        kc[rows, :] = kp[jj][...]
        vc[rows, :] = vp[jj][...]
    kcat[past:past + TP, :] = kin_ref[0][:, :IDX_DIM]
    kc[past:past + TP, :] = kn_ref[0]
    vc[past:past + TP, :] = vn_ref[0]
    kcat[past + TP:, :] = jnp.zeros((LP - past - TP, IDX_DIM), F32)
    kc[past + TP:, :] = jnp.zeros((LP - past - TP, AT_KV), F32)
    vc[past + TP:, :] = jnp.zeros((LP - past - TP, AT_KV), F32)

    R = T * IDX_HEADS
    qi = (qi_ref[0] * np.float32(IDX_DIM ** -0.5)).astype(BF16)
    dots = jnp.maximum(_dot_nt(qi, kcat[...].astype(BF16)), 0.0)
    wsc = dots * (wi_ref[0] * np.float32(IDX_HEADS ** -0.5))
    sc = jnp.sum(wsc.reshape(T, IDX_HEADS, LP), axis=1)
    colpos = lax.broadcasted_iota(I32, (T, LP), 1)
    rowpos = past + lax.broadcasted_iota(I32, (T, LP), 0)
    causal = colpos <= rowpos
    key = jnp.where(causal, _sort_key(sc), INT_MIN)

    def count(mask):
        return jnp.sum(jnp.where(mask, 1.0, 0.0), axis=1, keepdims=True).astype(I32)

    thr = _kth_largest_key(lambda cand: count(key >= cand), T, topk)
    n_gt = count(key > thr)
    tie = key == thr
    lim = _tie_limit(lambda cand: count(tie & (colpos < cand)), topk - n_gt, T, LP)
    tie_bias = jnp.where(tie, jnp.where(colpos <= lim, 0.0, -jnp.inf), -jnp.inf)
    bias = jnp.where(causal, jnp.where(key > thr, 0.0, tie_bias), -jnp.inf)
    bias = jnp.broadcast_to(bias[:, None, :], (T, AT_HEADS, LP)).reshape(T * AT_HEADS, LP)

    qbd = (qbd_ref[0] * np.float32(AT_HEAD_DIM ** -0.5)).astype(BF16)
    s = _dot_nt(qbd, kc[...].astype(BF16)) + bias
    m = jnp.max(s, axis=-1, keepdims=True)
    pr = jnp.exp(s - m)
    den = jnp.sum(pr, axis=-1, keepdims=True)
    o_ref[0] = _dot(pr.astype(BF16), vc[...].astype(BF16)) / den


def dsa_sample(page_table, layer, qi_s, wi_s, q_bd, k_new, v_new, kw_new, pool_k, pool_v, pool_i, *, T):
    DB, n_pages = page_table.shape
    past = n_pages * PAGE_SIZE
    LP = past + PAGE_SIZE
    topk = min(TOPK_MAX, (past + T) // 4)
    per_b = lambda a: pl.BlockSpec((1,) + a.shape[1:], lambda b, pt, l: (b,) + (0,) * (a.ndim - 1))

    def page(width, jj):
        return pl.BlockSpec((None, None, PAGE_SIZE, width), lambda b, pt, l, jj=jj: (l[0], pt[b, jj], 0, 0))

    in_specs = [per_b(a) for a in (qi_s, wi_s, q_bd, k_new, v_new, kw_new)]
    in_specs += [page(AT_KV, jj) for jj in range(n_pages)]
    in_specs += [page(AT_KV, jj) for jj in range(n_pages)]
    in_specs += [page(IDX_DIM, jj) for jj in range(n_pages)]
    grid_spec = pltpu.PrefetchScalarGridSpec(
        num_scalar_prefetch=2,
        grid=(DB,),
        in_specs=in_specs,
        out_specs=pl.BlockSpec((1, T * AT_HEADS, AT_KV), lambda b, pt, l: (b, 0, 0)),
        scratch_shapes=[pltpu.VMEM((LP, IDX_DIM), F32), pltpu.VMEM((LP, AT_KV), F32), pltpu.VMEM((LP, AT_KV), F32)],
    )
    return pl.pallas_call(
        functools.partial(_dsa_sample_kernel, n_pages=n_pages, T=T, topk=topk),
        grid_spec=grid_spec,
        out_shape=jax.ShapeDtypeStruct((DB, T * AT_HEADS, AT_KV), F32),
        compiler_params=_params("arbitrary"),
        name="dsa_sample",
    )(page_table, layer, qi_s, wi_s, q_bd, k_new, v_new, kw_new,
      *([pool_k] * n_pages), *([pool_v] * n_pages), *([pool_i] * n_pages))


def _layer_norm(y, g, b):
    mu = jnp.mean(y, axis=-1, keepdims=True)
    yc = y - mu
    var = jnp.mean(yc * yc, axis=-1, keepdims=True)
    return yc * lax.rsqrt(var + LN_EPS) * g + b


def _group_allreduce(x, op, lane):
    d = 1
    while d < GROUP_SIZE:
        up = pltpu.roll(x, LANES - d, axis=1)
        dn = pltpu.roll(x, d, axis=1)
        x = op(x, jnp.where((lane & d) == 0, up, dn))
        d *= 2
    return x


def _mix_out_kernel(orw_ref, oat_ref, gate_ref, x_ref, wrw_ref, wat_ref, wout_ref, lng_ref, lnb_ref,
                    rw_ref, rb_ref, x1_ref, eidx_ref, gw_ref, *, alpha):
    d = x_ref.shape[1]
    tm = x_ref.shape[0]
    a = _dot(orw_ref[...].astype(BF16), wrw_ref[...])
    b = _dot(oat_ref[...].astype(BF16), wat_ref[...])
    merged = gate_ref[:, :d] * a + gate_ref[:, d:] * b
    y = alpha * x_ref[...] + _dot(merged.astype(BF16), wout_ref[...])
    x1 = _layer_norm(y, lng_ref[...], lnb_ref[...])
    x1_ref[...] = x1

    lane = lax.broadcasted_iota(I32, (tm, LANES), 1)
    lanef = lane.astype(F32)
    real = lane < N_EXPERTS
    ninf = -jnp.inf
    far = np.float32(4 * LANES)
    s = _sigmoid(_dot(x1, rw_ref[...], precision=HIGHEST))
    sb = jnp.where(real, s + rb_ref[...], ninf)
    m1 = _group_allreduce(sb, jnp.maximum, lane)
    first = _group_allreduce(jnp.where(sb == m1, lanef, far), jnp.minimum, lane)
    m2 = _group_allreduce(jnp.where(lanef == first, ninf, sb), jnp.maximum, lane)
    gs = jnp.where(real, m1 + m2, ninf)
    gid = (lane >> 3).astype(F32)
    picked = jnp.zeros((tm, LANES), F32)
    for _ in range(TOPK_GROUPS):
        mx = jnp.max(gs, axis=1, keepdims=True)
        gfirst = jnp.min(jnp.where(gs == mx, gid, far), axis=1, keepdims=True)
        hit = gid == gfirst
        picked = jnp.where(hit, 1.0, picked)
        gs = jnp.where(hit, ninf, gs)
    cand = jnp.where((picked > 0.0) & real, sb, ninf)
    eidx = jnp.zeros((tm, LANES), F32)
    gw = jnp.zeros((tm, LANES), F32)
    for kk in range(TOP_K):
        mx = jnp.max(cand, axis=1, keepdims=True)
        idx = jnp.min(jnp.where(cand == mx, lanef, far), axis=1, keepdims=True)
        hit = lanef == idx
        wk = jnp.sum(jnp.where(hit, s, 0.0), axis=1, keepdims=True)
        eidx = jnp.where(lane == kk, idx, eidx)
        gw = jnp.where(lane == kk, wk, gw)
        cand = jnp.where(hit, ninf, cand)
    eidx_ref[...] = eidx.astype(I32)
    gw_ref[...] = gw / jnp.sum(gw, axis=1, keepdims=True) * np.float32(ROUTED_SCALE)


def mix_out(o_rw, o_at, gate, x, w_rw, w_at, w_out, ln_g, ln_b, router_w, router_b, *, tm, alpha):
    n, d = x.shape
    row = lambda w: pl.BlockSpec((tm, w), lambda i: (i, 0))
    full = lambda a: pl.BlockSpec(a.shape, lambda i: (0,) * a.ndim)
    return pl.pallas_call(
        functools.partial(_mix_out_kernel, alpha=alpha),
        grid=(n // tm,),
        in_specs=[row(RW_DIM), row(AT_Q), row(2 * d), row(d), full(w_rw), full(w_at), full(w_out),
                  full(ln_g), full(ln_b), full(router_w), full(router_b)],
        out_specs=[row(d), row(LANES), row(LANES)],
        out_shape=[jax.ShapeDtypeStruct((n, d), F32), jax.ShapeDtypeStruct((n, LANES), I32),
                   jax.ShapeDtypeStruct((n, LANES), F32)],
        compiler_params=_params("parallel"),
        name="mix_out_route",
    )(o_rw, o_at, gate, x, w_rw, w_at, w_out, ln_g, ln_b, router_w, router_b)


def _expert_kernel(be_ref, nu_ref, l_ref, x_ref, w1_ref, w3_ref, w2_ref, o_ref):
    i = pl.program_id(0)

    @pl.when(i < nu_ref[0])
    def _():
        x = x_ref[...]
        h1 = _dot(x, w1_ref[...])
        h3 = _dot(x, w3_ref[...])
        h = h1 * _sigmoid(h1) * h3
        o_ref[...] = _dot(h.astype(BF16), w2_ref[...])

    @pl.when(i >= nu_ref[0])
    def _():
        o_ref[...] = jnp.zeros(o_ref.shape, F32)


def expert_ffn(block_exp, n_used, layer, xg, w1, w3, w2, *, BM):
    n_slots, d = xg.shape
    ff = w1.shape[-1]
    grid_spec = pltpu.PrefetchScalarGridSpec(
        num_scalar_prefetch=3,
        grid=(n_slots // BM,),
        in_specs=[pl.BlockSpec((BM, d), lambda i, be, nu, l: (i, 0)),
                  pl.BlockSpec((None, None, d, ff), lambda i, be, nu, l: (l[0], be[i], 0, 0)),
                  pl.BlockSpec((None, None, d, ff), lambda i, be, nu, l: (l[0], be[i], 0, 0)),
                  pl.BlockSpec((None, None, ff, d), lambda i, be, nu, l: (l[0], be[i], 0, 0))],
        out_specs=pl.BlockSpec((BM, d), lambda i, be, nu, l: (i, 0)),
    )
    return pl.pallas_call(
        _expert_kernel,
        grid_spec=grid_spec,
        out_shape=jax.ShapeDtypeStruct((n_slots, d), F32),
        compiler_params=_params("arbitrary"),
        name="expert_ffn",
    )(block_exp, n_used, layer, xg, w1, w3, w2)


def _shared_kernel(x_ref, r_ref, w1_ref, w3_ref, w2_ref, lng_ref, lnb_ref, o_ref, *, alpha):
    x = x_ref[...]
    xb = x.astype(BF16)
    h1 = _dot(xb, w1_ref[...])
    h3 = _dot(xb, w3_ref[...])
    sh = _dot((h1 * _sigmoid(h1) * h3).astype(BF16), w2_ref[...])
    o_ref[...] = _layer_norm(alpha * x + (r_ref[...] + sh), lng_ref[...], lnb_ref[...])


def shared_ln(x, routed, w1, w3, w2, ln_g, ln_b, *, tm, alpha):
    n, d = x.shape
    row = pl.BlockSpec((tm, d), lambda i: (i, 0))
    full = lambda a: pl.BlockSpec(a.shape, lambda i: (0,) * a.ndim)
    return pl.pallas_call(
        functools.partial(_shared_kernel, alpha=alpha),
        grid=(n // tm,),
        in_specs=[row, row, full(w1), full(w3), full(w2), full(ln_g), full(ln_b)],
        out_specs=row,
        out_shape=jax.ShapeDtypeStruct((n, d), F32),
        compiler_params=_params("parallel"),
        name="shared_ln2",
    )(x, routed, w1, w3, w2, ln_g, ln_b)


def routed_experts(x1, eidx, gw, layer, w1, w3, w2, *, BM):
    n, d = x1.shape
    A = n * TOP_K
    e_flat = eidx.reshape(A)
    order = jnp.argsort(e_flat)
    e_sorted = e_flat[order]
    tok_sorted = (order // TOP_K).astype(I32)
    counts = jnp.bincount(e_flat, length=N_EXPERTS).astype(I32)
    padded = (counts + BM - 1) // BM * BM
    pad_end = jnp.cumsum(padded)
    pad_start = pad_end - padded
    start = jnp.cumsum(counts) - counts
    dest_sorted = pad_start[e_sorted] + jnp.arange(A, dtype=I32) - start[e_sorted]
    n_blocks = -(-A // BM) + N_EXPERTS
    slot_tok = jnp.full((n_blocks * BM,), n, I32).at[dest_sorted].set(tok_sorted)
    block_exp = jnp.minimum(jnp.searchsorted(pad_end, jnp.arange(n_blocks, dtype=I32) * BM, side='right'),
                            N_EXPERTS - 1).astype(I32)
    n_used = (pad_end[-1:] // BM).astype(I32)
    dest = jnp.zeros((A,), I32).at[order].set(dest_sorted).reshape(n, TOP_K)
    xpad = jnp.concatenate([x1.astype(BF16), jnp.zeros((1, d), BF16)], axis=0)
    yg = expert_ffn(block_exp, n_used, layer, xpad[slot_tok], w1, w3, w2, BM=BM)
    return jnp.sum(yg[dest] * gw[:, :, None], axis=1)


def kernel(x_prompt, x_sample, cache_k, cache_v, cache_idx, state_rwkv, state_shift, page_table, w_in, b_gate, rw_mu, rw_w2, rw_w0, rw_a2, rw_a0, rw_g2, rw_kk, rw_ka, rw_rk, rw_gn_g, rw_gn_b, rw_proj, at_proj, w_out, ln1_g, ln1_b, router_w, router_b, ex_w1, ex_w3, ex_w2, sh_w1, sh_w3, sh_w2, ln2_g, ln2_b):
    B, S, d = x_prompt.shape
    DB, T, _ = x_sample.shape
    depth = w_in.shape[0]
    n_pages = page_table.shape[1]
    past = n_pages * PAGE_SIZE
    n_pool = cache_k.shape[1]
    NP = B * S
    NS = DB * T
    N = NP + NS
    alpha = float((2 * depth) ** 0.25)
    TM = 256
    QB = 256
    CH = 64
    TP = SUBLANES * (-(-T // SUBLANES))
    BM = 256
    assert NP % TM == 0 and NS % TM == 0 and S % TM == 0 and S % QB == 0 and S % CH == 0

    split = np.cumsum([RW_COLS, AT_Q, AT_KV, AT_KV, IDX_HEADS * IDX_DIM, IDX_DIM, IDX_HEADS])
    w_rw, w_q, w_k, w_v, w_qi, w_ki, w_wi, w_g = jnp.split(w_in, split.tolist(), axis=-1)
    w_all = jnp.concatenate(
        [w_rw, w_q, w_k, w_v, w_qi, w_ki, w_wi,
         jnp.zeros((depth, d, LANES - IDX_DIM - IDX_HEADS), F32), w_g], axis=-1).astype(BF16)
    zlr = jnp.zeros((depth, RW_DECAY_RANK, RW_DIM), F32)
    wwa = jnp.concatenate([jnp.concatenate([rw_w2, zlr], -1), jnp.concatenate([zlr, rw_a2], -1)], 1).astype(BF16)
    g2 = rw_g2.astype(BF16)
    vecs = jnp.stack([rw_w0, rw_a0, rw_kk, rw_ka, rw_rk.reshape(depth, RW_DIM), rw_gn_g, rw_gn_b,
                      jnp.zeros_like(rw_w0)], axis=1)
    w_rwp, w_atp, w_o = rw_proj.astype(BF16), at_proj.astype(BF16), w_out.astype(BF16)
    r_w = jnp.concatenate([router_w, jnp.zeros((depth, d, LANES - N_EXPERTS), F32)], -1)
    r_b = jnp.concatenate([router_b, jnp.zeros((depth, LANES - N_EXPERTS), F32)], -1)[:, None, :]
    e1, e3, e2 = ex_w1.astype(BF16), ex_w3.astype(BF16), ex_w2.astype(BF16)
    s1, s3, s2 = sh_w1.astype(BF16), sh_w3.astype(BF16), sh_w2.astype(BF16)
    pool_k = cache_k.reshape(depth, n_pool, PAGE_SIZE, AT_KV)
    pool_v = cache_v.reshape(depth, n_pool, PAGE_SIZE, AT_KV)
    kv_eye = jnp.repeat(jnp.eye(AT_KV_HEADS, dtype=F32), AT_GROUP, axis=0)

    cos_t, sin_t = rope_table(S, past, T, TM)
    zero_state = jnp.zeros((B, RW_HEADS, RW_HEAD_DIM, RW_HEAD_DIM), F32)
    zero_row = jnp.zeros((B, 1, RW_COLS), F32)

    x = jnp.concatenate([x_prompt.reshape(NP, d), x_sample.reshape(NS, d)], axis=0)
    outs = {k: [] for k in ("kp", "vp", "ip", "rp", "sp", "ks", "vs", "is", "rs", "ss")}
    for l in range(depth):
        layer = jnp.full((1,), l, I32)
        outs["sp"].append(x[:NP].reshape(B, S, d)[:, -1])
        outs["ss"].append(x[NP:].reshape(DB, T, d)[:, -1])
        p_rw, q, k, v, qi, kw, gate, k_bf, v_bf, kw_bf = project(
            x, w_all[l], b_gate[l][None, :], cos_t, sin_t, tm=TM, n_prompt=NP, seq=S)
        outs["kp"].append(k[:NP].reshape(B, S, AT_KV_HEADS, AT_HEAD_DIM))
        outs["vp"].append(v[:NP].reshape(B, S, AT_KV_HEADS, AT_HEAD_DIM))
        outs["ip"].append(kw[:NP, :IDX_DIM].reshape(B, S, IDX_DIM))
        outs["ks"].append(k[NP:].reshape(DB, T, AT_KV_HEADS, AT_HEAD_DIM))
        outs["vs"].append(v[NP:].reshape(DB, T, AT_KV_HEADS, AT_HEAD_DIM))
        outs["is"].append(kw[NP:, :IDX_DIM].reshape(DB, T, IDX_DIM))

        rw_args = (rw_mu[l][None, :], wwa[l], g2[l], vecs[l])
        pp = p_rw[:NP].reshape(B, S, RW_COLS)
        pp_prev = jnp.concatenate([zero_row, pp[:, :-1]], axis=1)
        o_rw_p, st_p = rwkv7(pp, pp_prev, zero_state, *rw_args, C=CH, valid=CH)
        p0 = shift_project(state_shift[l], w_all[l])
        ps = p_rw[NP:].reshape(DB, T, RW_COLS)
        ps_prev = jnp.concatenate([p0[:, None], ps[:, :-1]], axis=1)
        padt = ((0, 0), (0, TP - T), (0, 0))
        o_rw_s, st_s = rwkv7(jnp.pad(ps, padt), jnp.pad(ps_prev, padt), state_rwkv[l], *rw_args, C=TP, valid=T)
        outs["rp"].append(st_p)
        outs["rs"].append(st_s)
        o_rw = jnp.concatenate([o_rw_p.reshape(NP, RW_DIM), o_rw_s[:, :T].reshape(NS, RW_DIM)], axis=0)

        seq3 = lambda a: a[:NP].reshape(B, S, a.shape[-1])
        o_at_p = dsa_prompt(seq3(q), seq3(qi), seq3(kw), seq3(k_bf), seq3(v_bf), seq3(kw_bf), QB=QB)
        q_s = q[NP:].reshape(DB, T, AT_HEADS, 1, AT_HEAD_DIM)
        q_bd = (q_s * kv_eye[None, None, :, :, None]).reshape(DB, T * AT_HEADS, AT_KV)
        qi_s = qi[NP:].reshape(DB, T * IDX_HEADS, IDX_DIM)
        wi_s = kw[NP:, IDX_DIM:IDX_DIM + IDX_HEADS].reshape(DB, T * IDX_HEADS, 1)
        new = lambda a: jnp.pad(a[NP:].reshape(DB, T, a.shape[-1]), padt)
        o_bd = dsa_sample(page_table, layer, qi_s, wi_s, q_bd, new(k), new(v), new(kw),
                          pool_k, pool_v, cache_idx, T=T)
        o_at_s = jnp.sum(o_bd.reshape(DB, T, AT_HEADS, AT_KV_HEADS, AT_HEAD_DIM) * kv_eye[None, None, :, :, None], axis=3)
        o_at = jnp.concatenate([o_at_p.reshape(NP, AT_Q), o_at_s.reshape(NS, AT_Q)], axis=0)

        x1, eidx, gw = mix_out(o_rw, o_at, gate, x, w_rwp[l], w_atp[l], w_o[l], ln1_g[l][None, :], ln1_b[l][None, :],
                               r_w[l], r_b[l], tm=TM, alpha=alpha)
        routed = routed_experts(x1, eidx[:, :TOP_K], gw[:, :TOP_K], layer, e1, e3, e2, BM=BM)
        x = shared_ln(x1, routed, s1[l], s3[l], s2[l], ln2_g[l][None, :], ln2_b[l][None, :], tm=TM, alpha=alpha)

    st = lambda name: jnp.stack(outs[name])
    return (x[:NP].reshape(B, S, d), x[NP:].reshape(DB, T, d),
            st("kp"), st("vp"), st("ip"), st("rp"), st("sp"),
            st("ks"), st("vs"), st("is"), st("rs"), st("ss"))
```

```python
import functools

import numpy as np
import jax
import jax.numpy as jnp
from jax import lax
from jax.experimental import pallas as pl
from jax.experimental.pallas import tpu as pltpu

F32 = jnp.float32
BF16 = jnp.bfloat16
I32 = jnp.int32
HIGHEST = lax.Precision.HIGHEST

RW_HEADS = 8
RW_HEAD_DIM = 64
RW_DIM = RW_HEADS * RW_HEAD_DIM
RW_DECAY_RANK = 64
RW_AAA_RANK = 64
RW_GATE_RANK = 128
RW_GN_EPS = 64e-5
RW_COLS = 3 * RW_DIM + RW_DECAY_RANK + RW_AAA_RANK + RW_GATE_RANK

AT_HEADS = 8
AT_KV_HEADS = 4
AT_HEAD_DIM = 64
AT_GROUP = AT_HEADS // AT_KV_HEADS
AT_Q = AT_HEADS * AT_HEAD_DIM
AT_KV = AT_KV_HEADS * AT_HEAD_DIM
IDX_HEADS = 8
IDX_DIM = 64
TOPK_MAX = 256
PAGE_SIZE = 128
ROPE_THETA = 10000.0

N_EXPERTS = 64
N_GROUPS = 8
GROUP_SIZE = N_EXPERTS // N_GROUPS
TOPK_GROUPS = 4
TOP_K = 8
EXPERT_FF = 256
SHARED_FF = 256
ROUTED_SCALE = 2.5
LN_EPS = 1e-5

LANES = 128
SUBLANES = 8
VMEM_LIMIT = 56 * 1024 * 1024

C_RW = 0
C_Q = C_RW + RW_COLS
C_K = C_Q + AT_Q
C_V = C_K + AT_KV
C_QI = C_V + AT_KV
C_KW = C_QI + IDX_HEADS * IDX_DIM
C_GATE = C_KW + LANES
INT_MIN = np.int32(-2 ** 31)


def _dot(a, b, **kw):
    return jnp.dot(a, b, preferred_element_type=F32, **kw)


def _dot_nt(a, b):
    return lax.dot_general(a, b, (((1,), (1,)), ((), ())), preferred_element_type=F32)


def _dot_tn(a, b):
    return lax.dot_general(a, b, (((0,), (0,)), ((), ())), preferred_element_type=F32)


def _params(*sem):
    return pltpu.CompilerParams(dimension_semantics=sem, vmem_limit_bytes=VMEM_LIMIT)


def _sigmoid(x):
    return 1.0 / (1.0 + jnp.exp(-x))


def _lane_tile(x, reps):
    return jnp.concatenate([x] * reps, axis=1) if reps > 1 else x


def _sort_key(s):
    s = jnp.where(s == 0.0, 0.0, s)
    bits = lax.bitcast_convert_type(s, I32)
    return bits ^ ((bits >> 31) & np.int32(0x7FFFFFFF))


def _rope_table_kernel(inv_ref, cos_ref, sin_ref, *, tm, seq, past, dec_seq):
    i = pl.program_id(0)
    row = i * tm + lax.broadcasted_iota(I32, (tm, LANES), 0)
    pos = jnp.where(row < seq, row, past + lax.rem(jnp.maximum(row - seq, 0), dec_seq))
    ang = pos.astype(F32) * inv_ref[...]
    lane = lax.broadcasted_iota(I32, (tm, LANES), 1)
    s = jnp.sin(ang)
    cos_ref[...] = jnp.cos(ang)
    sin_ref[...] = jnp.where((lane & (AT_HEAD_DIM - 1)) < AT_HEAD_DIM // 2, -s, s)


def rope_table(seq, past, dec_seq, tm):
    half = AT_HEAD_DIM // 2
    inv = (np.float32(ROPE_THETA) ** (-np.arange(half, dtype=np.float32) * np.float32(2.0 / AT_HEAD_DIM))).astype(np.float32)
    inv_row = jnp.asarray(np.tile(inv, LANES // half)[None, :])
    rows = seq + tm
    out = jax.ShapeDtypeStruct((rows, LANES), F32)
    return pl.pallas_call(
        functools.partial(_rope_table_kernel, tm=tm, seq=seq, past=past, dec_seq=dec_seq),
        grid=(rows // tm,),
        in_specs=[pl.BlockSpec((1, LANES), lambda i: (0, 0))],
        out_specs=[pl.BlockSpec((tm, LANES), lambda i: (i, 0))] * 2,
        out_shape=[out, out],
        compiler_params=_params("parallel"),
        name="rope_table",
    )(inv_row)


def _rope(x, cos, sin_signed, lo):
    up = pltpu.roll(x, LANES - AT_HEAD_DIM // 2, axis=1)
    dn = pltpu.roll(x, AT_HEAD_DIM // 2, axis=1)
    return x * cos + jnp.where(lo, up, dn) * sin_signed


def _proj_kernel(x_ref, w_ref, bg_ref, cos_ref, sin_ref,
                 prw_ref, q_ref, k_ref, v_ref, qi_ref, kw_ref, gate_ref, kbf_ref, vbf_ref, kwbf_ref):
    xb = x_ref[...].astype(BF16)
    tm = xb.shape[0]
    cos = cos_ref[...]
    sin = sin_ref[...]
    lane = lax.broadcasted_iota(I32, (tm, LANES), 1)
    lo = (lane & (AT_HEAD_DIM - 1)) < AT_HEAD_DIM // 2

    prw_ref[...] = _dot(xb, w_ref[:, C_RW:C_Q])

    def roped(c0, width, out_ref, bf_ref=None):
        p = _dot(xb, w_ref[:, c0:c0 + width])
        for g in range(width // LANES):
            sl = slice(g * LANES, (g + 1) * LANES)
            y = _rope(p[:, sl], cos, sin, lo)
            out_ref[:, sl] = y
            if bf_ref is not None:
                bf_ref[:, sl] = y.astype(BF16)

    roped(C_Q, AT_Q, q_ref)
    roped(C_K, AT_KV, k_ref, kbf_ref)
    roped(C_QI, IDX_HEADS * IDX_DIM, qi_ref)
    v = _dot(xb, w_ref[:, C_V:C_QI])
    v_ref[...] = v
    vbf_ref[...] = v.astype(BF16)
    kw = _dot(xb, w_ref[:, C_KW:C_GATE])
    kw = jnp.where(lane < IDX_DIM, _rope(kw, cos, sin, lo), kw)
    kw_ref[...] = kw
    kwbf_ref[...] = kw.astype(BF16)
    gate_ref[...] = _sigmoid(_dot(xb, w_ref[:, C_GATE:]) + bg_ref[...])


def project(x, w_all, b_gate, cos_t, sin_t, *, tm, n_prompt, seq):
    n, d = x.shape
    n_tab = seq // tm
    npt = n_prompt // tm

    def tab_map(i):
        return (jnp.where(i < npt, lax.rem(i, n_tab), n_tab), 0)

    row = lambda w: pl.BlockSpec((tm, w), lambda i: (i, 0))
    outs = [(RW_COLS, F32), (AT_Q, F32), (AT_KV, F32), (AT_KV, F32), (IDX_HEADS * IDX_DIM, F32), (LANES, F32),
            (2 * d, F32), (AT_KV, BF16), (AT_KV, BF16), (LANES, BF16)]
    return pl.pallas_call(
        _proj_kernel,
        grid=(n // tm,),
        in_specs=[row(d),
                  pl.BlockSpec(w_all.shape, lambda i: (0, 0)),
                  pl.BlockSpec((1, 2 * d), lambda i: (0, 0)),
                  pl.BlockSpec((tm, LANES), tab_map),
                  pl.BlockSpec((tm, LANES), tab_map)],
        out_specs=[row(w) for w, _ in outs],
        out_shape=[jax.ShapeDtypeStruct((n, w), dt) for w, dt in outs],
        compiler_params=_params("parallel"),
        name="in_proj",
    )(x, w_all, b_gate, cos_t, sin_t)


def _mm_kernel(x_ref, w_ref, o_ref):
    o_ref[...] = _dot(x_ref[...].astype(BF16), w_ref[...])


def shift_project(rows, w_all):
    m, d = rows.shape
    return pl.pallas_call(
        _mm_kernel,
        grid=(1,),
        in_specs=[pl.BlockSpec((m, d), lambda i: (0, 0)), pl.BlockSpec((d, RW_COLS), lambda i: (0, 0))],
        out_specs=pl.BlockSpec((m, RW_COLS), lambda i: (0, 0)),
        out_shape=jax.ShapeDtypeStruct((m, RW_COLS), F32),
        compiler_params=_params("arbitrary"),
        name="shift_proj",
    )(rows, w_all)


def _wkv_kernel(p_ref, pp_ref, s0_ref, mu_ref, wwa_ref, g2_ref, vec_ref, o_ref, sfin_ref, s_scr, *, C, valid):
    j = pl.program_id(1)
    D = RW_DIM
    N = RW_HEAD_DIM

    @pl.when(j == 0)
    def _():
        s_scr[...] = s0_ref[0]

    p = p_ref[0]
    xs = p + mu_ref[...] * (pp_ref[0] - p)
    r = xs[:, 0:D]
    k = xs[:, D:2 * D]
    v = xs[:, 2 * D:3 * D]
    x_lr = xs[:, 3 * D:3 * D + LANES]
    lane = lax.broadcasted_iota(I32, (C, LANES), 1)
    lr_in = jnp.where(lane < RW_DECAY_RANK, jnp.tanh(x_lr), x_lr)
    lr = _dot(lr_in.astype(BF16), wwa_ref[...])
    w0, a0, kkw, ka, rk, gng, gnb = (vec_ref[i:i + 1, :] for i in range(7))
    z = -(w0 + lr[:, :D])
    softplus = jnp.maximum(z, 0.0) + jnp.log1p(jnp.exp(-jnp.abs(z)))
    logw = -jnp.exp(-softplus - 0.5)
    if valid < C:
        rowid = lax.broadcasted_iota(I32, (C, D), 0)
        logw = jnp.where(rowid < valid, logw, 0.0)
    a_sig = _sigmoid(a0 + lr[:, D:])
    g = _dot(_sigmoid(xs[:, 3 * D + LANES:]).astype(BF16), g2_ref[...])
    kkv = k * kkw
    k2 = k * (1.0 + (a_sig - 1.0) * ka)
    bonus_in = r * k2 * rk

    ri = lax.broadcasted_iota(I32, (C, C), 0)
    ci = lax.broadcasted_iota(I32, (C, C), 1)
    strict = ri > ci
    incl = ri >= ci
    cl = _dot(incl.astype(F32), logw, precision=HIGHEST)
    e_pos = jnp.exp(cl)
    e_neg = jnp.exp(-cl)
    e_prev = jnp.exp(cl - logw)
    rt = r * e_pos
    kt = k2 * e_neg
    p_end = e_pos[C - 1:C, :]

    for h in range(RW_HEADS):
        sl = slice(h * N, (h + 1) * N)
        kk = kkv[:, sl]
        kk = kk / jnp.maximum(jnp.sqrt(jnp.sum(kk * kk, axis=-1, keepdims=True)), 1e-12)
        at = -kk * e_prev[:, sl]
        bt = kk * a_sig[:, sl] * e_neg[:, sl]
        ar = jnp.concatenate([at, rt[:, sl]], axis=0).astype(BF16)
        bk = jnp.concatenate([bt, kt[:, sl]], axis=0).astype(BF16)
        pm = _dot_nt(ar, bk)
        a_ab = jnp.where(strict, pm[:C, :C], 0.0)
        a_ak = jnp.where(strict, pm[:C, C:], 0.0)
        a_rb = jnp.where(incl, pm[C:, :C], 0.0)
        a_rk = jnp.where(incl, pm[C:, C:], 0.0)
        nn = a_ab
        lp = a_ab
        for _ in range(max(C.bit_length() - 2, 0)):
            lpb = lp.astype(BF16)
            lp = _dot(lpb, lpb)
            nn = nn + lp + _dot(nn.astype(BF16), lp.astype(BF16))
        s0 = s_scr[h]
        vh = v[:, sl]
        x0 = _dot_nt(ar, s0.astype(BF16))
        av = _dot(jnp.concatenate([a_ak, a_rk], axis=0).astype(BF16), vh.astype(BF16))
        rhs = x0[:C] + av[:C]
        u = rhs + _dot(nn.astype(BF16), rhs.astype(BF16))
        y = x0[C:] + av[C:] + _dot(a_rb.astype(BF16), u.astype(BF16))
        uv = jnp.concatenate([u, vh], axis=0).astype(BF16)
        s_scr[h] = (s0 + _dot_tn(uv, bk)) * p_end[:, sl]

        ym = jnp.mean(y, axis=-1, keepdims=True)
        yc = y - ym
        yv = jnp.mean(yc * yc, axis=-1, keepdims=True)
        yn = yc * lax.rsqrt(yv + RW_GN_EPS) * gng[:, sl] + gnb[:, sl]
        bonus = jnp.sum(bonus_in[:, sl], axis=-1, keepdims=True) * vh
        o_ref[0, :, sl] = (yn + bonus) * g[:, sl]

    @pl.when(j == pl.num_programs(1) - 1)
    def _():
        sfin_ref[0] = s_scr[...]


def rwkv7(p, p_prev, s0, mu, wwa, g2, vecs, *, C, valid):
    nseq, T, _ = p.shape
    tok = pl.BlockSpec((1, C, RW_COLS), lambda b, j: (b, j, 0))
    st = pl.BlockSpec((1, RW_HEADS, RW_HEAD_DIM, RW_HEAD_DIM), lambda b, j: (b, 0, 0, 0))
    full = lambda a: pl.BlockSpec(a.shape, lambda b, j: (0,) * a.ndim)
    return pl.pallas_call(
        functools.partial(_wkv_kernel, C=C, valid=valid),
        grid=(nseq, T // C),
        in_specs=[tok, tok, st, full(mu), full(wwa), full(g2), full(vecs)],
        out_specs=[pl.BlockSpec((1, C, RW_DIM), lambda b, j: (b, j, 0)), st],
        out_shape=[jax.ShapeDtypeStruct((nseq, T, RW_DIM), F32), jax.ShapeDtypeStruct(s0.shape, F32)],
        scratch_shapes=[pltpu.VMEM((RW_HEADS, RW_HEAD_DIM, RW_HEAD_DIM), F32)],
        compiler_params=_params("parallel", "arbitrary"),
        name="rwkv7_chunk%d" % C,
    )(p, p_prev, s0, mu, wwa, g2, vecs)


def _kth_largest_key(count_ge, rows, topk):
    def body(i, t):
        cand = t + jnp.left_shift(np.int32(1), 31 - i)
        return jnp.where(count_ge(cand) >= topk, cand, t)
    return lax.fori_loop(0, 32, body, jnp.full((rows, 1), INT_MIN, I32))


def _tie_limit(count_tie_below, need, rows, ncols):
    nbits = max((ncols - 1).bit_length(), 1)

    def body(i, x):
        cand = x + jnp.left_shift(np.int32(1), nbits - 1 - i)
        return jnp.where(count_tie_below(cand) < need, cand, x)
    return lax.fori_loop(0, nbits, body, jnp.zeros((rows, 1), I32))


def _dsa_prompt_kernel(qbd_ref, qip_ref, kwq_ref, kbf_ref, vbf_ref, kibf_ref, o_ref,
                       key_scr, jl_scr, w_scr, m_scr, l_scr, acc_scr, *, QB, topk, S):
    j = pl.program_id(1)
    nkb = j + 1
    rowpos = j * QB + lax.broadcasted_iota(I32, (QB, 1), 0)
    col0 = lax.broadcasted_iota(I32, (QB, QB), 1)
    reps = QB // LANES

    def cols(kb):
        return pl.ds(pl.multiple_of(kb * QB, QB), QB)

    kwq = kwq_ref[0]
    for h in range(IDX_HEADS):
        w_scr[h] = jnp.broadcast_to(kwq[:, IDX_DIM + h:IDX_DIM + h + 1] * np.float32(IDX_HEADS ** -0.5), (QB, LANES))

    def score_block(kb, _):
        ki = kibf_ref[0, cols(kb), :]
        sc = jnp.zeros((QB, QB), F32)
        for h in range(IDX_HEADS):
            dots = _dot_nt(qip_ref[0, :, h * LANES:(h + 1) * LANES], ki)
            sc = sc + jnp.maximum(dots, 0.0) * _lane_tile(w_scr[h], reps)
        key = jnp.where(kb * QB + col0 <= rowpos, _sort_key(sc), INT_MIN)
        key_scr[:, cols(kb)] = key
        return 0
    lax.fori_loop(0, nkb, score_block, 0)

    def count(pred):
        def blk(kb, acc):
            kblk = key_scr[:, cols(kb)]
            for c in range(reps):
                acc = acc + jnp.where(pred(kblk[:, c * LANES:(c + 1) * LANES], kb * QB + c * LANES), 1.0, 0.0)
            return acc
        acc = lax.fori_loop(0, nkb, blk, jnp.zeros((QB, LANES), F32))
        return jnp.sum(acc, axis=1, keepdims=True).astype(I32)

    thr = _kth_largest_key(lambda cand: count(lambda kv, c0: kv >= cand), QB, topk)
    n_gt = count(lambda kv, c0: kv > thr)
    n_ge = count(lambda kv, c0: kv >= thr)
    excess = (n_ge > topk) & (thr > INT_MIN)
    jl_scr[...] = jnp.full((QB, 1), S, I32)

    @pl.when(jnp.max(jnp.where(excess, 1.0, 0.0)) > 0.5)
    def _():
        lane0 = lax.broadcasted_iota(I32, (QB, LANES), 1)
        lim = _tie_limit(lambda cand: count(lambda kv, c0: (kv == thr) & (c0 + lane0 < cand)), topk - n_gt, QB, S)
        jl_scr[...] = jnp.where(excess, lim, S)

    jlim = jl_scr[...]

    m_scr[...] = jnp.full(m_scr.shape, -jnp.inf, F32)
    l_scr[...] = jnp.zeros(l_scr.shape, F32)
    acc_scr[...] = jnp.zeros(acc_scr.shape, F32)

    def attend_block(kb, _):
        kblk = key_scr[:, cols(kb)]
        colpos = kb * QB + col0
        tie_bias = jnp.where(kblk == thr, jnp.where(colpos <= jlim, 0.0, -jnp.inf), -jnp.inf)
        bias = jnp.where(colpos <= rowpos, jnp.where(kblk > thr, 0.0, tie_bias), -jnp.inf)
        kk = kbf_ref[0, cols(kb), :]
        vv = vbf_ref[0, cols(kb), :]
        for h in range(AT_HEADS):
            half = (h // AT_GROUP) // 2
            s = _dot_nt(qbd_ref[0, :, h * AT_KV:(h + 1) * AT_KV], kk) + bias
            m_old = m_scr[h]
            m_new = jnp.maximum(m_old, jnp.max(s, axis=1, keepdims=True))
            m_use = jnp.where(m_new == -jnp.inf, 0.0, m_new)
            pr = jnp.exp(s - _lane_tile(m_use, reps))
            alpha = jnp.exp(m_old - m_use)
            l_scr[h] = alpha * l_scr[h] + jnp.sum(pr, axis=1, keepdims=True)
            acc_scr[h] = alpha * acc_scr[h] + _dot(pr.astype(BF16), vv[:, half * LANES:(half + 1) * LANES])
            m_scr[h] = m_new
        return 0
    lax.fori_loop(0, nkb, attend_block, 0)

    for h in range(AT_HEADS):
        off = ((h // AT_GROUP) % 2) * AT_HEAD_DIM
        o_ref[0, :, h * AT_HEAD_DIM:(h + 1) * AT_HEAD_DIM] = (acc_scr[h] / l_scr[h])[:, off:off + AT_HEAD_DIM]


def dsa_prompt(q_bd, qi_pad, kw, k_bf, v_bf, kw_bf, *, QB):
    B, S, _ = kw.shape
    topk = min(TOPK_MAX, S // 4)
    qblk = lambda w: pl.BlockSpec((1, QB, w), lambda b, j: (b, j, 0))
    seq = lambda w: pl.BlockSpec((1, S, w), lambda b, j: (b, 0, 0))
    hstate = pltpu.VMEM((AT_HEADS, QB, LANES), F32)
    return pl.pallas_call(
        functools.partial(_dsa_prompt_kernel, QB=QB, topk=topk, S=S),
        grid=(B, S // QB),
        in_specs=[qblk(AT_HEADS * AT_KV), qblk(IDX_HEADS * LANES), qblk(LANES), seq(AT_KV), seq(AT_KV), seq(LANES)],
        out_specs=qblk(AT_Q),
        out_shape=jax.ShapeDtypeStruct((B, S, AT_Q), F32),
        scratch_shapes=[pltpu.VMEM((QB, S), I32), pltpu.VMEM((QB, 1), I32),
                        pltpu.VMEM((IDX_HEADS, QB, LANES), F32), hstate, hstate, hstate],
        compiler_params=_params("parallel", "arbitrary"),
        name="dsa_prompt",
    )(q_bd, qi_pad, kw, k_bf, v_bf, kw_bf)


def _dsa_sample_kernel(pt_ref, l_ref, qi_ref, wi_ref, qbd_ref, kn_ref, vn_ref, kin_ref, *rest, n_pages, T, topk):
    kp = rest[:n_pages]
    vp = rest[n_pages:2 * n_pages]
    ip = rest[2 * n_pages:3 * n_pages]
    o_ref, kcat, kc, vc = rest[3 * n_pages:]
    past = n_pages * PAGE_SIZE
    LP = past + PAGE_SIZE
    TP = kn_ref.shape[1]
    for jj in range(n_pages):
        rows = slice(jj * PAGE_SIZE, (jj + 1) * PAGE_SIZE)
        kcat[rows, :] = ip[jj][...]
        kc[rows, :] = kp[jj][...]
        vc[rows, :] = vp[jj][...]
    kcat[past:past + TP, :] = kin_ref[0][:, :IDX_DIM]
    kc[past:past + TP, :] = kn_ref[0]
    vc[past:past + TP, :] = vn_ref[0]
    kcat[past + TP:, :] = jnp.zeros((LP - past - TP, IDX_DIM), F32)
    kc[past + TP:, :] = jnp.zeros((LP - past - TP, AT_KV), F32)
    vc[past + TP:, :] = jnp.zeros((LP - past - TP, AT_KV), F32)

    R = T * IDX_HEADS
    qi = (qi_ref[0] * np.float32(IDX_DIM ** -0.5)).astype(BF16)
    dots = jnp.maximum(_dot_nt(qi, kcat[...].astype(BF16)), 0.0)
    wsc = dots * (wi_ref[0] * np.float32(IDX_HEADS ** -0.5))
    sc = jnp.sum(wsc.reshape(T, IDX_HEADS, LP), axis=1)
    colpos = lax.broadcasted_iota(I32, (T, LP), 1)
    rowpos = past + lax.broadcasted_iota(I32, (T, LP), 0)
    causal = colpos <= rowpos
    key = jnp.where(causal, _sort_key(sc), INT_MIN)

    def count(mask):
        return jnp.sum(jnp.where(mask, 1.0, 0.0), axis=1, keepdims=True).astype(I32)

    thr = _kth_largest_key(lambda cand: count(key >= cand), T, topk)
    n_gt = count(key > thr)
    tie = key == thr
    lim = _tie_limit(lambda cand: count(tie & (colpos < cand)), topk - n_gt, T, LP)
    tie_bias = jnp.where(tie, jnp.where(colpos <= lim, 0.0, -jnp.inf), -jnp.inf)
    bias = jnp.where(causal, jnp.where(key > thr, 0.0, tie_bias), -jnp.inf)
    bias = jnp.broadcast_to(bias[:, None, :], (T, AT_HEADS, LP)).reshape(T * AT_HEADS, LP)

    qbd = (qbd_ref[0] * np.float32(AT_HEAD_DIM ** -0.5)).astype(BF16)
    s = _dot_nt(qbd, kc[...].astype(BF16)) + bias
    m = jnp.max(s, axis=-1, keepdims=True)
    pr = jnp.exp(s - m)
    den = jnp.sum(pr, axis=-1, keepdims=True)
    o_ref[0] = _dot(pr.astype(BF16), vc[...].astype(BF16)) / den


def dsa_sample(page_table, layer, qi_s, wi_s, q_bd, k_new, v_new, kw_new, pool_k, pool_v, pool_i, *, T):
    DB, n_pages = page_table.shape
    past = n_pages * PAGE_SIZE
    LP = past + PAGE_SIZE
    topk = min(TOPK_MAX, (past + T) // 4)
    per_b = lambda a: pl.BlockSpec((1,) + a.shape[1:], lambda b, pt, l: (b,) + (0,) * (a.ndim - 1))

    def page(width, jj):
        return pl.BlockSpec((None, None, PAGE_SIZE, width), lambda b, pt, l, jj=jj: (l[0], pt[b, jj], 0, 0))

    in_specs = [per_b(a) for a in (qi_s, wi_s, q_bd, k_new, v_new, kw_new)]
    in_specs += [page(AT_KV, jj) for jj in range(n_pages)]
    in_specs += [page(AT_KV, jj) for jj in range(n_pages)]
    in_specs += [page(IDX_DIM, jj) for jj in range(n_pages)]
    grid_spec = pltpu.PrefetchScalarGridSpec(
        num_scalar_prefetch=2,
        grid=(DB,),
        in_specs=in_specs,
        out_specs=pl.BlockSpec((1, T * AT_HEADS, AT_KV), lambda b, pt, l: (b, 0, 0)),
        scratch_shapes=[pltpu.VMEM((LP, IDX_DIM), F32), pltpu.VMEM((LP, AT_KV), F32), pltpu.VMEM((LP, AT_KV), F32)],
    )
    return pl.pallas_call(
        functools.partial(_dsa_sample_kernel, n_pages=n_pages, T=T, topk=topk),
        grid_spec=grid_spec,
        out_shape=jax.ShapeDtypeStruct((DB, T * AT_HEADS, AT_KV), F32),
        compiler_params=_params("arbitrary"),
        name="dsa_sample",
    )(page_table, layer, qi_s, wi_s, q_bd, k_new, v_new, kw_new,
      *([pool_k] * n_pages), *([pool_v] * n_pages), *([pool_i] * n_pages))


def _layer_norm(y, g, b):
    mu = jnp.mean(y, axis=-1, keepdims=True)
    yc = y - mu
    var = jnp.mean(yc * yc, axis=-1, keepdims=True)
    return yc * lax.rsqrt(var + LN_EPS) * g + b


def _group_allreduce(x, op, lane):
    d = 1
    while d < GROUP_SIZE:
        up = pltpu.roll(x, LANES - d, axis=1)
        dn = pltpu.roll(x, d, axis=1)
        x = op(x, jnp.where((lane & d) == 0, up, dn))
        d *= 2
    return x


def _mix_out_kernel(orw_ref, oat_ref, gate_ref, x_ref, wrw_ref, wat_ref, wout_ref, lng_ref, lnb_ref,
                    rw_ref, rb_ref, x1_ref, x1b_ref, rank_ref, gatew_ref, cnt_ref, *, alpha):
    d = x_ref.shape[1]
    tm = x_ref.shape[0]
    a = _dot(orw_ref[...].astype(BF16), wrw_ref[...])
    b = _dot(oat_ref[...].astype(BF16), wat_ref[...])
    merged = gate_ref[:, :d] * a + gate_ref[:, d:] * b
    y = alpha * x_ref[...] + _dot(merged.astype(BF16), wout_ref[...])
    x1 = _layer_norm(y, lng_ref[...], lnb_ref[...])
    x1_ref[...] = x1
    x1b_ref[...] = x1.astype(BF16)

    lane = lax.broadcasted_iota(I32, (tm, LANES), 1)
    lanef = lane.astype(F32)
    real = lane < N_EXPERTS
    ninf = -jnp.inf
    far = np.float32(4 * LANES)
    s = _sigmoid(_dot(x1, rw_ref[...], precision=HIGHEST))
    sb = jnp.where(real, s + rb_ref[...], ninf)
    m1 = _group_allreduce(sb, jnp.maximum, lane)
    first = _group_allreduce(jnp.where(sb == m1, lanef, far), jnp.minimum, lane)
    m2 = _group_allreduce(jnp.where(lanef == first, ninf, sb), jnp.maximum, lane)
    gs = jnp.where(real, m1 + m2, ninf)
    gid = (lane >> 3).astype(F32)
    picked = jnp.zeros((tm, LANES), F32)
    for _ in range(TOPK_GROUPS):
        mx = jnp.max(gs, axis=1, keepdims=True)
        gfirst = jnp.min(jnp.where(gs == mx, gid, far), axis=1, keepdims=True)
        hit = gid == gfirst
        picked = jnp.where(hit, 1.0, picked)
        gs = jnp.where(hit, ninf, gs)
    cand = jnp.where((picked > 0.0) & real, sb, ninf)
    sel = jnp.zeros((tm, LANES), F32)
    for _ in range(TOP_K):
        mx = jnp.max(cand, axis=1, keepdims=True)
        idx = jnp.min(jnp.where(cand == mx, lanef, far), axis=1, keepdims=True)
        hit = lanef == idx
        sel = jnp.where(hit, 1.0, sel)
        cand = jnp.where(hit, ninf, cand)
    gate_w = sel * s
    gate_w = gate_w / jnp.sum(gate_w, axis=1, keepdims=True) * np.float32(ROUTED_SCALE)
    ri = lax.broadcasted_iota(I32, (tm, tm), 0)
    ci = lax.broadcasted_iota(I32, (tm, tm), 1)
    before = jnp.where(ri > ci, 1.0, 0.0).astype(BF16)
    rank = jnp.where(sel > 0.0, _dot(before, sel.astype(BF16)), -1.0)
    rank_ref[0] = rank.T
    gatew_ref[0] = gate_w.T
    cnt_ref[0] = jnp.sum(sel, axis=0, keepdims=True)


def mix_out(o_rw, o_at, gate, x, w_rw, w_at, w_out, ln_g, ln_b, router_w, router_b, *, tm, alpha):
    n, d = x.shape
    nt = n // tm
    row = lambda w: pl.BlockSpec((tm, w), lambda i: (i, 0))
    full = lambda a: pl.BlockSpec(a.shape, lambda i: (0,) * a.ndim)
    tab = pl.BlockSpec((1, LANES, tm), lambda i: (i, 0, 0))
    return pl.pallas_call(
        functools.partial(_mix_out_kernel, alpha=alpha),
        grid=(nt,),
        in_specs=[row(RW_DIM), row(AT_Q), row(2 * d), row(d), full(w_rw), full(w_at), full(w_out),
                  full(ln_g), full(ln_b), full(router_w), full(router_b)],
        out_specs=[row(d), row(d), tab, tab, pl.BlockSpec((1, 1, LANES), lambda i: (i, 0, 0))],
        out_shape=[jax.ShapeDtypeStruct((n, d), F32), jax.ShapeDtypeStruct((n, d), BF16),
                   jax.ShapeDtypeStruct((nt, LANES, tm), F32), jax.ShapeDtypeStruct((nt, LANES, tm), F32),
                   jax.ShapeDtypeStruct((nt, 1, LANES), F32)],
        compiler_params=_params("parallel"),
        name="mix_out_route",
    )(o_rw, o_at, gate, x, w_rw, w_at, w_out, ln_g, ln_b, router_w, router_b)


CELL = 16
WIN = 64


def _expert_kernel(be_ref, i0_ref, i1_ref, nu_ref, cs_ref, cl_ref, l_ref,
                   rank_ref, gatew_ref, x_hbm, w1_ref, w3_ref, w2_ref, o_ref,
                   xbuf, sem, xblk, gblk, pairs, *, BM, tm, nt):
    b = pl.program_id(0)
    n_used = nu_ref[0]

    def tile_copy(tile, slot):
        return pltpu.make_async_copy(x_hbm.at[pl.ds(pl.multiple_of(tile * tm, tm), tm), :], xbuf.at[slot], sem.at[slot])

    @pl.when((b == 0) & (n_used > 0))
    def _():
        pairs[0] = 0
        tile_copy(i0_ref[0], 0).start()

    @pl.when(b < n_used)
    def _():
        e = be_ref[b]
        base = b * BM
        last = i1_ref[b]
        xblk[...] = jnp.zeros(xblk.shape, BF16)
        gblk[...] = jnp.zeros(gblk.shape, F32)
        riota = lax.broadcasted_iota(I32, (WIN, tm), 0).astype(F32)

        def pair(i, _):
            n_done = pairs[0]
            slot = lax.rem(n_done, 2)
            in_block = i < last
            nxt = jnp.where(in_block, i + 1, i0_ref[jnp.minimum(b + 1, pl.num_programs(0) - 1)])

            @pl.when(in_block | (b + 1 < n_used))
            def _():
                tile_copy(nxt, 1 - slot).start()

            tile_copy(i, slot).wait()
            pairs[0] = n_done + 1
            off = cs_ref[e * nt + i] - base
            lo = jnp.maximum(off, 0)
            hi = jnp.minimum(off + cl_ref[e * nt + i], BM)
            rank_row = rank_ref[pl.ds(i, 1), :]
            gate_row = gatew_ref[pl.ds(i, 1), :]
            xt = xbuf[slot]

            def window(c, _):
                w = pl.multiple_of(lo + WIN * c, CELL)
                match = rank_row == riota + (w - off).astype(F32)
                onehot = jnp.where(match, 1.0, 0.0).astype(BF16)
                xblk[pl.ds(w, WIN), :] = _dot(onehot, xt).astype(BF16)
                gw = jnp.sum(jnp.where(match, gate_row, 0.0), axis=1, keepdims=True)
                gblk[pl.ds(w, WIN), :] = jnp.broadcast_to(gw, (WIN, LANES))
                return 0
            lax.fori_loop(0, jnp.maximum((hi - lo + WIN - 1) // WIN, 0), window, 0)
            return 0
        lax.fori_loop(i0_ref[b], last + 1, pair, 0)

        x = xblk[0:BM, :]
        h1 = _dot(x, w1_ref[...])
        h3 = _dot(x, w3_ref[...])
        h = h1 * _sigmoid(h1) * h3
        y = _dot(h.astype(BF16), w2_ref[...])
        o_ref[...] = y * _lane_tile(gblk[0:BM, :], y.shape[1] // LANES)

    @pl.when(b >= n_used)
    def _():
        o_ref[...] = jnp.zeros(o_ref.shape, F32)


def expert_ffn(block_exp, i0, i1, n_used, cell_start, cell_len, layer, rank_e, gatew_e, x1b, w1, w3, w2,
               *, BM, tm, n_blocks):
    d = x1b.shape[1]
    ff = w1.shape[-1]
    nt_pad = rank_e.shape[1]
    nt = x1b.shape[0] // tm
    sp = lambda f: (lambda b, be, i0, i1, nu, cs, cl, l: f(b, be, l))
    grid_spec = pltpu.PrefetchScalarGridSpec(
        num_scalar_prefetch=7,
        grid=(n_blocks,),
        in_specs=[pl.BlockSpec((None, nt_pad, tm), sp(lambda b, be, l: (be[b], 0, 0))),
                  pl.BlockSpec((None, nt_pad, tm), sp(lambda b, be, l: (be[b], 0, 0))),
                  pl.BlockSpec(memory_space=pl.ANY),
                  pl.BlockSpec((None, None, d, ff), sp(lambda b, be, l: (l[0], be[b], 0, 0))),
                  pl.BlockSpec((None, None, d, ff), sp(lambda b, be, l: (l[0], be[b], 0, 0))),
                  pl.BlockSpec((None, None, ff, d), sp(lambda b, be, l: (l[0], be[b], 0, 0)))],
        out_specs=pl.BlockSpec((BM, d), sp(lambda b, be, l: (b, 0))),
        scratch_shapes=[pltpu.VMEM((2, tm, d), BF16), pltpu.SemaphoreType.DMA((2,)),
                        pltpu.VMEM((BM + WIN, d), BF16), pltpu.VMEM((BM + WIN, LANES), F32),
                        pltpu.SMEM((1,), I32)],
    )
    return pl.pallas_call(
        functools.partial(_expert_kernel, BM=BM, tm=tm, nt=nt),
        grid_spec=grid_spec,
        out_shape=jax.ShapeDtypeStruct((n_blocks * BM, d), F32),
        compiler_params=_params("arbitrary"),
        name="expert_ffn",
    )(block_exp, i0, i1, n_used, cell_start, cell_len, layer, rank_e, gatew_e, x1b, w1, w3, w2)


COMBINE_GROUP = 8


def _combine_kernel(cs_ref, nr_ref, rank_ref, x_ref, y_hbm, w1_ref, w3_ref, w2_ref, lng_ref, lnb_ref, o_ref,
                    ybuf, sem, acc, *, alpha):
    i = pl.program_id(0)
    tm = x_ref.shape[0]
    n_groups = N_EXPERTS // COMBINE_GROUP
    riota = lax.broadcasted_iota(I32, (WIN, tm), 0).astype(F32)
    acc[...] = jnp.zeros(acc.shape, F32)

    def one_round(c, _):
        def window_copy(g, k, slot):
            start = pl.multiple_of(cs_ref[i * N_EXPERTS + g * COMBINE_GROUP + k] + WIN * c, CELL)
            return pltpu.make_async_copy(y_hbm.at[pl.ds(start, WIN), :], ybuf.at[slot, k * WIN:(k + 1) * WIN, :],
                                         sem.at[slot])

        for k in range(COMBINE_GROUP):
            window_copy(0, k, 0).start()
        for g in range(n_groups):
            slot = g % 2
            if g + 1 < n_groups:
                for k in range(COMBINE_GROUP):
                    window_copy(g + 1, k, 1 - slot).start()
            for k in range(COMBINE_GROUP):
                window_copy(g, k, slot).wait()
            y = ybuf[slot]
            y_hi = y.astype(BF16)
            y_lo = (y - y_hi.astype(F32)).astype(BF16)
            target = riota + (WIN * c).astype(F32)
            onehot = jnp.concatenate(
                [jnp.where(rank_ref[0, g * COMBINE_GROUP + k:g * COMBINE_GROUP + k + 1, :] == target, 1.0, 0.0)
                 for k in range(COMBINE_GROUP)], axis=0).astype(BF16)
            acc[...] += _dot_tn(onehot, y_hi) + _dot_tn(onehot, y_lo)
        return 0
    lax.fori_loop(0, nr_ref[i], one_round, 0)

    x = x_ref[...]
    xb = x.astype(BF16)
    h1 = _dot(xb, w1_ref[...])
    h3 = _dot(xb, w3_ref[...])
    sh = _dot((h1 * _sigmoid(h1) * h3).astype(BF16), w2_ref[...])
    o_ref[...] = _layer_norm(alpha * x + (acc[...] + sh), lng_ref[...], lnb_ref[...])


def combine_shared_ln(cell_start_t, n_rounds, rank_t, x, yg, w1, w3, w2, ln_g, ln_b, *, tm, alpha):
    n, d = x.shape
    sp = lambda f: (lambda i, cs, nr: f(i))
    full = lambda a: pl.BlockSpec(a.shape, sp(lambda i: (0,) * a.ndim))
    row = pl.BlockSpec((tm, d), sp(lambda i: (i, 0)))
    grid_spec = pltpu.PrefetchScalarGridSpec(
        num_scalar_prefetch=2,
        grid=(n // tm,),
        in_specs=[pl.BlockSpec((1, LANES, tm), sp(lambda i: (i, 0, 0))), row, pl.BlockSpec(memory_space=pl.ANY),
                  full(w1), full(w3), full(w2), full(ln_g), full(ln_b)],
        out_specs=row,
        scratch_shapes=[pltpu.VMEM((2, COMBINE_GROUP * WIN, d), F32), pltpu.SemaphoreType.DMA((2,)),
                        pltpu.VMEM((tm, d), F32)],
    )
    return pl.pallas_call(
        functools.partial(_combine_kernel, alpha=alpha),
        grid_spec=grid_spec,
        out_shape=jax.ShapeDtypeStruct((n, d), F32),
        compiler_params=_params("arbitrary"),
        name="combine_shared_ln2",
    )(cell_start_t, n_rounds, rank_t, x, yg, w1, w3, w2, ln_g, ln_b)


def moe(x1, x1b, rank_t, gatew_t, cnt_t, layer, w1, w3, w2, s1, s3, s2, ln_g, ln_b, *, BM, tm, alpha):
    n, d = x1.shape
    nt = n // tm
    E = N_EXPERTS
    cnt = cnt_t[:, 0, :E].astype(I32)
    clen = (cnt + CELL - 1) // CELL * CELL
    region = (jnp.sum(clen, axis=0) + BM - 1) // BM * BM
    reg_end = jnp.cumsum(region)
    cstart = (reg_end - region)[None, :] + jnp.cumsum(clen, axis=0) - clen
    n_blocks = -(-(n * TOP_K + (CELL - 1) * nt * E + (BM - 1) * E) // BM) + 1
    blk0 = jnp.arange(n_blocks, dtype=I32) * BM
    block_exp = jnp.minimum(jnp.sum(reg_end[None, :] <= blk0[:, None], axis=1), E - 1).astype(I32)
    n_used = (reg_end[-1:] // BM).astype(I32)
    cs_b = cstart.T[block_exp]
    i0 = jnp.sum(cs_b + clen.T[block_exp] <= blk0[:, None], axis=1).astype(I32)
    i1 = (jnp.sum(cs_b < blk0[:, None] + BM, axis=1) - 1).astype(I32)
    n_rounds = jnp.maximum((jnp.max(clen, axis=1) + WIN - 1) // WIN, 0).astype(I32)
    nt_pad = -(-nt // SUBLANES) * SUBLANES
    e_major = lambda a: jnp.pad(jnp.transpose(a, (1, 0, 2))[:E], ((0, 0), (0, nt_pad - nt), (0, 0)))
    yg = expert_ffn(block_exp, i0, i1, n_used, cstart.T.reshape(-1).astype(I32), clen.T.reshape(-1).astype(I32), layer,
                    e_major(rank_t), e_major(gatew_t), x1b, w1, w3, w2, BM=BM, tm=tm, n_blocks=n_blocks)
    return combine_shared_ln(cstart.reshape(-1).astype(I32), n_rounds, rank_t, x1, yg, s1, s3, s2, ln_g, ln_b,
                             tm=tm, alpha=alpha)


def kernel(x_prompt, x_sample, cache_k, cache_v, cache_idx, state_rwkv, state_shift, page_table, w_in, b_gate, rw_mu, rw_w2, rw_w0, rw_a2, rw_a0, rw_g2, rw_kk, rw_ka, rw_rk, rw_gn_g, rw_gn_b, rw_proj, at_proj, w_out, ln1_g, ln1_b, router_w, router_b, ex_w1, ex_w3, ex_w2, sh_w1, sh_w3, sh_w2, ln2_g, ln2_b):
    B, S, d = x_prompt.shape
    DB, T, _ = x_sample.shape
    depth = w_in.shape[0]
    n_pages = page_table.shape[1]
    past = n_pages * PAGE_SIZE
    n_pool = cache_k.shape[1]
    NP = B * S
    NS = DB * T
    N = NP + NS
    alpha = float((2 * depth) ** 0.25)
    TM = 256
    QB = 256
    CH = 64
    TP = SUBLANES * (-(-T // SUBLANES))
    BM = 256
    assert NP % TM == 0 and NS % TM == 0 and S % TM == 0 and S % QB == 0 and S % CH == 0

    split = np.cumsum([RW_COLS, AT_Q, AT_KV, AT_KV, IDX_HEADS * IDX_DIM, IDX_DIM, IDX_HEADS])
    w_rw, w_q, w_k, w_v, w_qi, w_ki, w_wi, w_g = jnp.split(w_in, split.tolist(), axis=-1)
    w_all = jnp.concatenate(
        [w_rw, w_q, w_k, w_v, w_qi, w_ki, w_wi,
         jnp.zeros((depth, d, LANES - IDX_DIM - IDX_HEADS), F32), w_g], axis=-1).astype(BF16)
    zlr = jnp.zeros((depth, RW_DECAY_RANK, RW_DIM), F32)
    wwa = jnp.concatenate([jnp.concatenate([rw_w2, zlr], -1), jnp.concatenate([zlr, rw_a2], -1)], 1).astype(BF16)
    g2 = rw_g2.astype(BF16)
    vecs = jnp.stack([rw_w0, rw_a0, rw_kk, rw_ka, rw_rk.reshape(depth, RW_DIM), rw_gn_g, rw_gn_b,
                      jnp.zeros_like(rw_w0)], axis=1)
    w_rwp, w_atp, w_o = rw_proj.astype(BF16), at_proj.astype(BF16), w_out.astype(BF16)
    r_w = jnp.concatenate([router_w, jnp.zeros((depth, d, LANES - N_EXPERTS), F32)], -1)
    r_b = jnp.concatenate([router_b, jnp.zeros((depth, LANES - N_EXPERTS), F32)], -1)[:, None, :]
    e1, e3, e2 = ex_w1.astype(BF16), ex_w3.astype(BF16), ex_w2.astype(BF16)
    s1, s3, s2 = sh_w1.astype(BF16), sh_w3.astype(BF16), sh_w2.astype(BF16)
    pool_k = cache_k.reshape(depth, n_pool, PAGE_SIZE, AT_KV)
    pool_v = cache_v.reshape(depth, n_pool, PAGE_SIZE, AT_KV)
    kv_eye = jnp.repeat(jnp.eye(AT_KV_HEADS, dtype=F32), AT_GROUP, axis=0)

    cos_t, sin_t = rope_table(S, past, T, TM)
    zero_state = jnp.zeros((B, RW_HEADS, RW_HEAD_DIM, RW_HEAD_DIM), F32)
    zero_row = jnp.zeros((B, 1, RW_COLS), F32)

    x = jnp.concatenate([x_prompt.reshape(NP, d), x_sample.reshape(NS, d)], axis=0)
    outs = {k: [] for k in ("kp", "vp", "ip", "rp", "sp", "ks", "vs", "is", "rs", "ss")}
    for l in range(depth):
        layer = jnp.full((1,), l, I32)
        outs["sp"].append(x[:NP].reshape(B, S, d)[:, -1])
        outs["ss"].append(x[NP:].reshape(DB, T, d)[:, -1])
        p_rw, q, k, v, qi, kw, gate, k_bf, v_bf, kw_bf = project(
            x, w_all[l], b_gate[l][None, :], cos_t, sin_t, tm=TM, n_prompt=NP, seq=S)
        outs["kp"].append(k[:NP].reshape(B, S, AT_KV_HEADS, AT_HEAD_DIM))
        outs["vp"].append(v[:NP].reshape(B, S, AT_KV_HEADS, AT_HEAD_DIM))
        outs["ip"].append(kw[:NP, :IDX_DIM].reshape(B, S, IDX_DIM))
        outs["ks"].append(k[NP:].reshape(DB, T, AT_KV_HEADS, AT_HEAD_DIM))
        outs["vs"].append(v[NP:].reshape(DB, T, AT_KV_HEADS, AT_HEAD_DIM))
        outs["is"].append(kw[NP:, :IDX_DIM].reshape(DB, T, IDX_DIM))

        rw_args = (rw_mu[l][None, :], wwa[l], g2[l], vecs[l])
        pp = p_rw[:NP].reshape(B, S, RW_COLS)
        pp_prev = jnp.concatenate([zero_row, pp[:, :-1]], axis=1)
        o_rw_p, st_p = rwkv7(pp, pp_prev, zero_state, *rw_args, C=CH, valid=CH)
        p0 = shift_project(state_shift[l], w_all[l])
        ps = p_rw[NP:].reshape(DB, T, RW_COLS)
        ps_prev = jnp.concatenate([p0[:, None], ps[:, :-1]], axis=1)
        padt = ((0, 0), (0, TP - T), (0, 0))
        o_rw_s, st_s = rwkv7(jnp.pad(ps, padt), jnp.pad(ps_prev, padt), state_rwkv[l], *rw_args, C=TP, valid=T)
        outs["rp"].append(st_p)
        outs["rs"].append(st_s)
        o_rw = jnp.concatenate([o_rw_p.reshape(NP, RW_DIM), o_rw_s[:, :T].reshape(NS, RW_DIM)], axis=0)

        seq3 = lambda a: a[:NP].reshape(B, S, a.shape[-1])
        q_bd_p = (q[:NP].reshape(NP, AT_HEADS, 1, AT_HEAD_DIM) * (kv_eye * np.float32(AT_HEAD_DIM ** -0.5))[None, :, :, None]
                  ).astype(BF16).reshape(B, S, AT_HEADS * AT_KV)
        qi_pad = jnp.pad(qi[:NP].reshape(NP, IDX_HEADS, IDX_DIM) * np.float32(IDX_DIM ** -0.5),
                         ((0, 0), (0, 0), (0, LANES - IDX_DIM))).astype(BF16).reshape(B, S, IDX_HEADS * LANES)
        o_at_p = dsa_prompt(q_bd_p, qi_pad, seq3(kw), seq3(k_bf), seq3(v_bf), seq3(kw_bf), QB=QB)
        q_s = q[NP:].reshape(DB, T, AT_HEADS, 1, AT_HEAD_DIM)
        q_bd = (q_s * kv_eye[None, None, :, :, None]).reshape(DB, T * AT_HEADS, AT_KV)
        qi_s = qi[NP:].reshape(DB, T * IDX_HEADS, IDX_DIM)
        wi_s = kw[NP:, IDX_DIM:IDX_DIM + IDX_HEADS].reshape(DB, T * IDX_HEADS, 1)
        new = lambda a: jnp.pad(a[NP:].reshape(DB, T, a.shape[-1]), padt)
        o_bd = dsa_sample(page_table, layer, qi_s, wi_s, q_bd, new(k), new(v), new(kw),
                          pool_k, pool_v, cache_idx, T=T)
        o_at_s = jnp.sum(o_bd.reshape(DB, T, AT_HEADS, AT_KV_HEADS, AT_HEAD_DIM) * kv_eye[None, None, :, :, None], axis=3)
        o_at = jnp.concatenate([o_at_p.reshape(NP, AT_Q), o_at_s.reshape(NS, AT_Q)], axis=0)

        x1, x1b, rank_t, gatew_t, cnt_t = mix_out(o_rw, o_at, gate, x, w_rwp[l], w_atp[l], w_o[l], ln1_g[l][None, :],
                                                  ln1_b[l][None, :], r_w[l], r_b[l], tm=TM, alpha=alpha)
        x = moe(x1, x1b, rank_t, gatew_t, cnt_t, layer, e1, e3, e2, s1[l], s3[l], s2[l],
                ln2_g[l][None, :], ln2_b[l][None, :], BM=BM, tm=TM, alpha=alpha)

    st = lambda name: jnp.stack(outs[name])
    return (x[:NP].reshape(B, S, d), x[NP:].reshape(DB, T, d),
            st("kp"), st("vp"), st("ip"), st("rp"), st("sp"),
            st("ks"), st("vs"), st("is"), st("rs"), st("ss"))
```

```python
import functools

import numpy as np
import jax
import jax.numpy as jnp
from jax import lax
from jax.experimental import pallas as pl
from jax.experimental.pallas import tpu as pltpu

F32 = jnp.float32
BF16 = jnp.bfloat16
I32 = jnp.int32
HIGHEST = lax.Precision.HIGHEST

RW_HEADS = 8
RW_HEAD_DIM = 64
RW_DIM = RW_HEADS * RW_HEAD_DIM
RW_DECAY_RANK = 64
RW_AAA_RANK = 64
RW_GATE_RANK = 128
RW_GN_EPS = 64e-5
RW_COLS = 3 * RW_DIM + RW_DECAY_RANK + RW_AAA_RANK + RW_GATE_RANK

AT_HEADS = 8
AT_KV_HEADS = 4
AT_HEAD_DIM = 64
AT_GROUP = AT_HEADS // AT_KV_HEADS
AT_Q = AT_HEADS * AT_HEAD_DIM
AT_KV = AT_KV_HEADS * AT_HEAD_DIM
IDX_HEADS = 8
IDX_DIM = 64
TOPK_MAX = 256
PAGE_SIZE = 128
ROPE_THETA = 10000.0

N_EXPERTS = 64
N_GROUPS = 8
GROUP_SIZE = N_EXPERTS // N_GROUPS
TOPK_GROUPS = 4
TOP_K = 8
EXPERT_FF = 256
SHARED_FF = 256
ROUTED_SCALE = 2.5
LN_EPS = 1e-5

LANES = 128
SUBLANES = 8
VMEM_LIMIT = 56 * 1024 * 1024

C_RW = 0
C_Q = C_RW + RW_COLS
C_K = C_Q + AT_Q
C_V = C_K + AT_KV
C_QI = C_V + AT_KV
C_KW = C_QI + IDX_HEADS * IDX_DIM
C_GATE = C_KW + LANES
INT_MIN = np.int32(-2 ** 31)


def _dot(a, b, **kw):
    return jnp.dot(a, b, preferred_element_type=F32, **kw)


def _dot_nt(a, b):
    return lax.dot_general(a, b, (((1,), (1,)), ((), ())), preferred_element_type=F32)


def _dot_tn(a, b):
    return lax.dot_general(a, b, (((0,), (0,)), ((), ())), preferred_element_type=F32)


def _params(*sem):
    return pltpu.CompilerParams(dimension_semantics=sem, vmem_limit_bytes=VMEM_LIMIT)


def _sigmoid(x):
    return 1.0 / (1.0 + jnp.exp(-x))


def _lane_tile(x, reps):
    return jnp.concatenate([x] * reps, axis=1) if reps > 1 else x


def _sort_key(s):
    s = jnp.where(s == 0.0, 0.0, s)
    bits = lax.bitcast_convert_type(s, I32)
    return bits ^ ((bits >> 31) & np.int32(0x7FFFFFFF))


def _rope_table_kernel(inv_ref, cos_ref, sin_ref, *, tm, seq, past, dec_seq):
    i = pl.program_id(0)
    row = i * tm + lax.broadcasted_iota(I32, (tm, LANES), 0)
    pos = jnp.where(row < seq, row, past + lax.rem(jnp.maximum(row - seq, 0), dec_seq))
    ang = pos.astype(F32) * inv_ref[...]
    lane = lax.broadcasted_iota(I32, (tm, LANES), 1)
    s = jnp.sin(ang)
    cos_ref[...] = jnp.cos(ang)
    sin_ref[...] = jnp.where((lane & (AT_HEAD_DIM - 1)) < AT_HEAD_DIM // 2, -s, s)


def rope_table(seq, past, dec_seq, tm):
    half = AT_HEAD_DIM // 2
    inv = (np.float32(ROPE_THETA) ** (-np.arange(half, dtype=np.float32) * np.float32(2.0 / AT_HEAD_DIM))).astype(np.float32)
    inv_row = jnp.asarray(np.tile(inv, LANES // half)[None, :])
    rows = seq + tm
    out = jax.ShapeDtypeStruct((rows, LANES), F32)
    return pl.pallas_call(
        functools.partial(_rope_table_kernel, tm=tm, seq=seq, past=past, dec_seq=dec_seq),
        grid=(rows // tm,),
        in_specs=[pl.BlockSpec((1, LANES), lambda i: (0, 0))],
        out_specs=[pl.BlockSpec((tm, LANES), lambda i: (i, 0))] * 2,
        out_shape=[out, out],
        compiler_params=_params("parallel"),
        name="rope_table",
    )(inv_row)


def _rope(x, cos, sin_signed, lo):
    up = pltpu.roll(x, LANES - AT_HEAD_DIM // 2, axis=1)
    dn = pltpu.roll(x, AT_HEAD_DIM // 2, axis=1)
    return x * cos + jnp.where(lo, up, dn) * sin_signed


def _proj_kernel(x_ref, w_ref, bg_ref, cos_ref, sin_ref,
                 prw_ref, q_ref, k_ref, v_ref, qi_ref, kw_ref, gate_ref, kbf_ref, vbf_ref, kwbf_ref):
    xb = x_ref[...].astype(BF16)
    tm = xb.shape[0]
    cos = cos_ref[...]
    sin = sin_ref[...]
    lane = lax.broadcasted_iota(I32, (tm, LANES), 1)
    lo = (lane & (AT_HEAD_DIM - 1)) < AT_HEAD_DIM // 2

    prw_ref[...] = _dot(xb, w_ref[:, C_RW:C_Q])

    def roped(c0, width, out_ref, bf_ref=None):
        p = _dot(xb, w_ref[:, c0:c0 + width])
        for g in range(width // LANES):
            sl = slice(g * LANES, (g + 1) * LANES)
            y = _rope(p[:, sl], cos, sin, lo)
            out_ref[:, sl] = y
            if bf_ref is not None:
                bf_ref[:, sl] = y.astype(BF16)

    roped(C_Q, AT_Q, q_ref)
    roped(C_K, AT_KV, k_ref, kbf_ref)
    roped(C_QI, IDX_HEADS * IDX_DIM, qi_ref)
    v = _dot(xb, w_ref[:, C_V:C_QI])
    v_ref[...] = v
    vbf_ref[...] = v.astype(BF16)
    kw = _dot(xb, w_ref[:, C_KW:C_GATE])
    kw = jnp.where(lane < IDX_DIM, _rope(kw, cos, sin, lo), kw)
    kw_ref[...] = kw
    kwbf_ref[...] = kw.astype(BF16)
    gate_ref[...] = _sigmoid(_dot(xb, w_ref[:, C_GATE:]) + bg_ref[...])


def project(x, w_all, b_gate, cos_t, sin_t, *, tm, n_prompt, seq):
    n, d = x.shape
    n_tab = seq // tm
    npt = n_prompt // tm

    def tab_map(i):
        return (jnp.where(i < npt, lax.rem(i, n_tab), n_tab), 0)

    row = lambda w: pl.BlockSpec((tm, w), lambda i: (i, 0))
    outs = [(RW_COLS, F32), (AT_Q, F32), (AT_KV, F32), (AT_KV, F32), (IDX_HEADS * IDX_DIM, F32), (LANES, F32),
            (2 * d, F32), (AT_KV, BF16), (AT_KV, BF16), (LANES, BF16)]
    return pl.pallas_call(
        _proj_kernel,
        grid=(n // tm,),
        in_specs=[row(d),
                  pl.BlockSpec(w_all.shape, lambda i: (0, 0)),
                  pl.BlockSpec((1, 2 * d), lambda i: (0, 0)),
                  pl.BlockSpec((tm, LANES), tab_map),
                  pl.BlockSpec((tm, LANES), tab_map)],
        out_specs=[row(w) for w, _ in outs],
        out_shape=[jax.ShapeDtypeStruct((n, w), dt) for w, dt in outs],
        compiler_params=_params("parallel"),
        name="in_proj",
    )(x, w_all, b_gate, cos_t, sin_t)


def _mm_kernel(x_ref, w_ref, o_ref):
    o_ref[...] = _dot(x_ref[...].astype(BF16), w_ref[...])


def shift_project(rows, w_all):
    m, d = rows.shape
    return pl.pallas_call(
        _mm_kernel,
        grid=(1,),
        in_specs=[pl.BlockSpec((m, d), lambda i: (0, 0)), pl.BlockSpec((d, RW_COLS), lambda i: (0, 0))],
        out_specs=pl.BlockSpec((m, RW_COLS), lambda i: (0, 0)),
        out_shape=jax.ShapeDtypeStruct((m, RW_COLS), F32),
        compiler_params=_params("arbitrary"),
        name="shift_proj",
    )(rows, w_all)


def _wkv_kernel(p_ref, pp_ref, s0_ref, mu_ref, wwa_ref, g2_ref, vec_ref, o_ref, sfin_ref, s_scr, *, G, C, valid):
    j = pl.program_id(1)
    D = RW_DIM
    N = RW_HEAD_DIM
    R = G * C

    @pl.when(j == 0)
    def _():
        s_scr[...] = s0_ref[...].reshape(G * RW_HEADS, N, N)

    p = p_ref[...].reshape(R, RW_COLS)
    xs = p + mu_ref[...] * (pp_ref[...].reshape(R, RW_COLS) - p)
    r = xs[:, 0:D]
    k = xs[:, D:2 * D]
    v = xs[:, 2 * D:3 * D]
    x_lr = xs[:, 3 * D:3 * D + LANES]
    lane = lax.broadcasted_iota(I32, (R, LANES), 1)
    lr_in = jnp.where(lane < RW_DECAY_RANK, jnp.tanh(x_lr), x_lr)
    lr = _dot(lr_in.astype(BF16), wwa_ref[...])
    w0, a0, kkw, ka, rk, gng, gnb = (vec_ref[i:i + 1, :] for i in range(7))
    z = -(w0 + lr[:, :D])
    softplus = jnp.maximum(z, 0.0) + jnp.log1p(jnp.exp(-jnp.abs(z)))
    logw = -jnp.exp(-softplus - 0.5)
    if valid < C:
        rowid = lax.broadcasted_iota(I32, (R, D), 0)
        logw = jnp.where(lax.rem(rowid, C) < valid, logw, 0.0)
    a_sig = _sigmoid(a0 + lr[:, D:])
    g = _dot(_sigmoid(xs[:, 3 * D + LANES:]).astype(BF16), g2_ref[...])
    kkv = k * kkw
    k2 = k * (1.0 + (a_sig - 1.0) * ka)
    bonus_in = r * k2 * rk

    rr = lax.broadcasted_iota(I32, (R, R), 0)
    cr = lax.broadcasted_iota(I32, (R, R), 1)
    same_seq = (rr >= cr) & (rr - lax.rem(rr, C) <= cr)
    cl = _dot(jnp.where(same_seq, 1.0, 0.0), logw, precision=HIGHEST)
    e_pos = jnp.exp(cl)
    e_neg = jnp.exp(-cl)
    e_prev = jnp.exp(cl - logw)
    rt = r * e_pos
    kt = k2 * e_neg
    ri = lax.broadcasted_iota(I32, (C, C), 0)
    ci = lax.broadcasted_iota(I32, (C, C), 1)
    strict = ri > ci
    incl = ri >= ci

    units = [(q, h) for q in range(G) for h in range(RW_HEADS)]
    rows = lambda q: slice(q * C, (q + 1) * C)
    lanes = lambda h: slice(h * N, (h + 1) * N)
    cut = lambda a, q, h: a[rows(q), lanes(h)]
    bf = lambda a: a.astype(BF16)
    kk = []
    for q, h in units:
        kq = cut(kkv, q, h)
        kk.append(kq / jnp.maximum(jnp.sqrt(jnp.sum(kq * kq, axis=-1, keepdims=True)), 1e-12))
    ar = [bf(jnp.concatenate([-kk[n] * cut(e_prev, q, h), cut(rt, q, h)], axis=0)) for n, (q, h) in enumerate(units)]
    bk = [bf(jnp.concatenate([kk[n] * cut(a_sig, q, h) * cut(e_neg, q, h), cut(kt, q, h)], axis=0))
          for n, (q, h) in enumerate(units)]
    vh = [cut(v, q, h) for q, h in units]
    s_old = [s_scr[n] for n in range(len(units))]
    pm = [_dot_nt(ar[n], bk[n]) for n in range(len(units))]
    x0 = [_dot_nt(ar[n], bf(s_old[n])) for n in range(len(units))]
    a_ab = [jnp.where(strict, m[:C, :C], 0.0) for m in pm]
    a_rb = [bf(jnp.where(incl, m[C:, :C], 0.0)) for m in pm]
    a_kk = [bf(jnp.concatenate([jnp.where(strict, m[:C, C:], 0.0), jnp.where(incl, m[C:, C:], 0.0)], axis=0)) for m in pm]
    av = [_dot(a_kk[n], bf(vh[n])) for n in range(len(units))]
    nn = a_ab
    lp = a_ab
    for _ in range(max(C.bit_length() - 2, 0)):
        lpb = [bf(m) for m in lp]
        lp = [_dot(m, m) for m in lpb]
        nn = [a + b + _dot(bf(a), bf(b)) for a, b in zip(nn, lp)]
    rhs = [x[:C] + a[:C] for x, a in zip(x0, av)]
    u = [a + _dot(bf(m), bf(a)) for a, m in zip(rhs, nn)]
    y = [x0[n][C:] + av[n][C:] + _dot(a_rb[n], bf(u[n])) for n in range(len(units))]
    ds = [_dot_tn(bf(jnp.concatenate([u[n], vh[n]], axis=0)), bk[n]) for n in range(len(units))]
    for n, (q, h) in enumerate(units):
        last = (q + 1) * C - 1
        s_scr[n] = (s_old[n] + ds[n]) * e_pos[last:last + 1, lanes(h)]
        ym = jnp.mean(y[n], axis=-1, keepdims=True)
        yc = y[n] - ym
        yv = jnp.mean(yc * yc, axis=-1, keepdims=True)
        yn = yc * lax.rsqrt(yv + RW_GN_EPS) * gng[:, lanes(h)] + gnb[:, lanes(h)]
        bonus = jnp.sum(cut(bonus_in, q, h), axis=-1, keepdims=True) * vh[n]
        o_ref[q, :, lanes(h)] = (yn + bonus) * cut(g, q, h)

    @pl.when(j == pl.num_programs(1) - 1)
    def _():
        sfin_ref[...] = s_scr[...].reshape(G, RW_HEADS, N, N)


def rwkv7(p, p_prev, s0, mu, wwa, g2, vecs, *, G, C, valid):
    nseq, T, _ = p.shape
    tok = pl.BlockSpec((G, C, RW_COLS), lambda b, j: (b, j, 0))
    st = pl.BlockSpec((G, RW_HEADS, RW_HEAD_DIM, RW_HEAD_DIM), lambda b, j: (b, 0, 0, 0))
    full = lambda a: pl.BlockSpec(a.shape, lambda b, j: (0,) * a.ndim)
    return pl.pallas_call(
        functools.partial(_wkv_kernel, G=G, C=C, valid=valid),
        grid=(nseq // G, T // C),
        in_specs=[tok, tok, st, full(mu), full(wwa), full(g2), full(vecs)],
        out_specs=[pl.BlockSpec((G, C, RW_DIM), lambda b, j: (b, j, 0)), st],
        out_shape=[jax.ShapeDtypeStruct((nseq, T, RW_DIM), F32), jax.ShapeDtypeStruct(s0.shape, F32)],
        scratch_shapes=[pltpu.VMEM((G * RW_HEADS, RW_HEAD_DIM, RW_HEAD_DIM), F32)],
        compiler_params=_params("parallel", "arbitrary"),
        name="rwkv7_chunk%d" % C,
    )(p, p_prev, s0, mu, wwa, g2, vecs)


def _kth_largest_key(count_ge, rows, topk):
    def body(i, t):
        cand = t + jnp.left_shift(np.int32(1), 31 - i)
        return jnp.where(count_ge(cand) >= topk, cand, t)
    return lax.fori_loop(0, 32, body, jnp.full((rows, 1), INT_MIN, I32))


def _tie_limit(count_tie_below, need, rows, ncols):
    nbits = max((ncols - 1).bit_length(), 1)

    def body(i, x):
        cand = x + jnp.left_shift(np.int32(1), nbits - 1 - i)
        return jnp.where(count_tie_below(cand) < need, cand, x)
    return lax.fori_loop(0, nbits, body, jnp.zeros((rows, 1), I32))


def _dsa_prompt_kernel(qbd_ref, qip_ref, kwq_ref, kbf_ref, vbf_ref, kibf_ref, o_ref,
                       key_scr, a_scr, b_scr, jl_scr, w_scr, m_scr, l_scr, acc_scr, *, QB, topk, S):
    j = pl.program_id(1)
    nkb = j + 1
    rowpos = j * QB + lax.broadcasted_iota(I32, (QB, 1), 0)
    col0 = lax.broadcasted_iota(I32, (QB, QB), 1)
    reps = QB // LANES

    def cols(kb):
        return pl.ds(pl.multiple_of(kb * QB, QB), QB)

    kwq = kwq_ref[0]
    for h in range(IDX_HEADS):
        w_scr[h] = jnp.broadcast_to(kwq[:, IDX_DIM + h:IDX_DIM + h + 1] * np.float32(IDX_HEADS ** -0.5), (QB, LANES))

    def score_block(kb, _):
        ki = kibf_ref[0, cols(kb), :]
        sc = jnp.zeros((QB, QB), F32)
        for h in range(IDX_HEADS):
            dots = _dot_nt(qip_ref[0, :, h * LANES:(h + 1) * LANES], ki)
            sc = sc + jnp.maximum(dots, 0.0) * _lane_tile(w_scr[h], reps)
        causal = kb * QB + col0 <= rowpos
        sc = jnp.where(sc == 0.0, 0.0, sc)
        key_scr[:, cols(kb)] = jnp.where(causal, _sort_key(sc), INT_MIN)
        a_scr[:, cols(kb)] = jnp.where(causal, sc, -jnp.inf).astype(BF16)
        return 0
    lax.fori_loop(0, nkb, score_block, 0)

    ones_kb = jnp.ones((QB, LANES), BF16)
    one_bf = jnp.ones((), BF16)
    zero_bf = jnp.zeros((), BF16)
    kf = np.float32(topk)

    def count_packed(src, cand, strict=False):
        cand_t = _lane_tile(cand.astype(BF16), reps)

        def blk(kb, acc):
            x = src[:, cols(kb)]
            hit = (x > cand_t) if strict else (x >= cand_t)
            return acc + jnp.where(hit, one_bf, zero_bf)
        hits = lax.fori_loop(0, nkb, blk, jnp.zeros((QB, QB), BF16))
        return _dot(hits, ones_kb)

    key16_neg_inf = np.int32(-32641)

    def key16_value(c):
        bits = c ^ ((c >> 15) & np.int32(0x7FFF))
        val = lax.bitcast_convert_type(jnp.left_shift(bits, 16), F32)
        return jnp.where(c <= key16_neg_inf, -jnp.inf, val)

    def stage_a(i, t):
        c = t + jnp.left_shift(np.int32(1), 15 - i)
        return jnp.where(count_packed(a_scr, key16_value(c)) >= kf, c, t)
    t16 = lax.fori_loop(0, 16, stage_a, jnp.full((QB, LANES), -32768, I32))
    thr_a = key16_value(t16)
    open_row = t16 <= key16_neg_inf
    need = kf - count_packed(a_scr, thr_a, strict=True)
    key_c = _sort_key(thr_a)
    key_c_t = _lane_tile(key_c, reps)
    thr_a_t = _lane_tile(thr_a, reps)

    def band_offset(kb):
        band = a_scr[:, cols(kb)].astype(F32) == thr_a_t
        return band, key_scr[:, cols(kb)] - key_c_t + np.int32(32768)

    def prep_hi(kb, _):
        band, d = band_offset(kb)
        b_scr[:, cols(kb)] = jnp.where(band, (d >> 8).astype(F32), -1.0).astype(BF16)
        return 0
    lax.fori_loop(0, nkb, prep_hi, 0)

    def stage_hi(i, t):
        c = t + jnp.left_shift(np.int32(1), 8 - i)
        ok = (count_packed(b_scr, c.astype(F32)) >= need) & (c <= 256)
        return jnp.where(ok, c, t)
    t_hi = lax.fori_loop(0, 9, stage_hi, jnp.zeros((QB, LANES), I32))
    need = need - count_packed(b_scr, t_hi.astype(F32), strict=True)
    t_hi_t = _lane_tile(t_hi, reps)

    def prep_lo(kb, _):
        band, d = band_offset(kb)
        b_scr[:, cols(kb)] = jnp.where(band & ((d >> 8) == t_hi_t), (d & 255).astype(F32), -1.0).astype(BF16)
        return 0
    lax.fori_loop(0, nkb, prep_lo, 0)

    def stage_lo(i, t):
        c = t + jnp.left_shift(np.int32(1), 7 - i)
        return jnp.where(count_packed(b_scr, c.astype(F32)) >= need, c, t)
    t_lo = lax.fori_loop(0, 8, stage_lo, jnp.zeros((QB, LANES), I32))
    n_gt_lo = count_packed(b_scr, t_lo.astype(F32), strict=True)
    n_ties = count_packed(b_scr, t_lo.astype(F32)) - n_gt_lo
    slots = need - n_gt_lo
    thr_l = jnp.where(open_row, INT_MIN, key_c - np.int32(32768) + jnp.left_shift(t_hi, 8) + t_lo)
    thr_t = _lane_tile(thr_l, reps)
    excess = (n_ties > slots) & jnp.logical_not(open_row)
    jl_scr[...] = jnp.full((QB, 1), S, I32)

    @pl.when(jnp.max(jnp.where(excess, 1.0, 0.0)) > 0.5)
    def _():
        thr1 = thr_l[:, :1]
        lane0 = lax.broadcasted_iota(I32, (QB, LANES), 1)

        def count_tie_below(cand):
            def blk(kb, acc):
                kblk = key_scr[:, cols(kb)]
                for c in range(reps):
                    hit = (kblk[:, c * LANES:(c + 1) * LANES] == thr1) & (kb * QB + c * LANES + lane0 < cand)
                    acc = acc + jnp.where(hit, 1.0, 0.0)
                return acc
            acc = lax.fori_loop(0, nkb, blk, jnp.zeros((QB, LANES), F32))
            return jnp.sum(acc, axis=1, keepdims=True).astype(I32)
        lim = _tie_limit(count_tie_below, slots[:, :1].astype(I32), QB, S)
        jl_scr[...] = jnp.where(excess[:, :1], lim, S)

    jlim = jl_scr[...]

    m_scr[...] = jnp.full(m_scr.shape, -jnp.inf, F32)
    l_scr[...] = jnp.zeros(l_scr.shape, F32)
    acc_scr[...] = jnp.zeros(acc_scr.shape, F32)

    def attend_block(kb, _):
        kblk = key_scr[:, cols(kb)]
        colpos = kb * QB + col0
        tie_bias = jnp.where(kblk == thr_t, jnp.where(colpos <= jlim, 0.0, -jnp.inf), -jnp.inf)
        bias = jnp.where(colpos <= rowpos, jnp.where(kblk > thr_t, 0.0, tie_bias), -jnp.inf)
        kk = kbf_ref[0, cols(kb), :]
        vv = vbf_ref[0, cols(kb), :]
        for h in range(AT_HEADS):
            half = (h // AT_GROUP) // 2
            s = _dot_nt(qbd_ref[0, :, h * AT_KV:(h + 1) * AT_KV], kk) + bias
            m_old = m_scr[h]
            m_new = jnp.maximum(m_old, jnp.max(s, axis=1, keepdims=True))
            m_use = jnp.where(m_new == -jnp.inf, 0.0, m_new)
            pr = jnp.exp(s - _lane_tile(m_use, reps))
            alpha = jnp.exp(m_old - m_use)
            l_scr[h] = alpha * l_scr[h] + jnp.sum(pr, axis=1, keepdims=True)
            acc_scr[h] = alpha * acc_scr[h] + _dot(pr.astype(BF16), vv[:, half * LANES:(half + 1) * LANES])
            m_scr[h] = m_new
        return 0
    lax.fori_loop(0, nkb, attend_block, 0)

    for h in range(AT_HEADS):
        off = ((h // AT_GROUP) % 2) * AT_HEAD_DIM
        o_ref[0, :, h * AT_HEAD_DIM:(h + 1) * AT_HEAD_DIM] = (acc_scr[h] / l_scr[h])[:, off:off + AT_HEAD_DIM]


def dsa_prompt(q_bd, qi_pad, kw, k_bf, v_bf, kw_bf, *, QB):
    B, S, _ = kw.shape
    topk = min(TOPK_MAX, S // 4)
    qblk = lambda w: pl.BlockSpec((1, QB, w), lambda b, j: (b, j, 0))
    seq = lambda w: pl.BlockSpec((1, S, w), lambda b, j: (b, 0, 0))
    hstate = pltpu.VMEM((AT_HEADS, QB, LANES), F32)
    return pl.pallas_call(
        functools.partial(_dsa_prompt_kernel, QB=QB, topk=topk, S=S),
        grid=(B, S // QB),
        in_specs=[qblk(AT_HEADS * AT_KV), qblk(IDX_HEADS * LANES), qblk(LANES), seq(AT_KV), seq(AT_KV), seq(LANES)],
        out_specs=qblk(AT_Q),
        out_shape=jax.ShapeDtypeStruct((B, S, AT_Q), F32),
        scratch_shapes=[pltpu.VMEM((QB, S), I32), pltpu.VMEM((QB, S), BF16), pltpu.VMEM((QB, S), BF16),
                        pltpu.VMEM((QB, 1), I32), pltpu.VMEM((IDX_HEADS, QB, LANES), F32), hstate, hstate, hstate],
        compiler_params=_params("parallel", "arbitrary"),
        name="dsa_prompt",
    )(q_bd, qi_pad, kw, k_bf, v_bf, kw_bf)


def _dsa_sample_kernel(pt_ref, l_ref, qi_ref, wi_ref, qbd_ref, kn_ref, vn_ref, kin_ref, *rest, n_pages, T, topk):
    kp = rest[:n_pages]
    vp = rest[n_pages:2 * n_pages]
    ip = rest[2 * n_pages:3 * n_pages]
    o_ref, kcat, kc, vc = rest[3 * n_pages:]
    past = n_pages * PAGE_SIZE
    LP = past + PAGE_SIZE
    TP = kn_ref.shape[1]
    for jj in range(n_pages):
        rows = slice(jj * PAGE_SIZE, (jj + 1) * PAGE_SIZE)
        kcat[rows, :] = ip[jj][...]
        kc[rows, :] = kp[jj][...]
        vc[rows, :] = vp[jj][...]
    kcat[past:past + TP, :] = kin_ref[0][:, :IDX_DIM]
    kc[past:past + TP, :] = kn_ref[0]
    vc[past:past + TP, :] = vn_ref[0]
    kcat[past + TP:, :] = jnp.zeros((LP - past - TP, IDX_DIM), F32)
    kc[past + TP:, :] = jnp.zeros((LP - past - TP, AT_KV), F32)
    vc[past + TP:, :] = jnp.zeros((LP - past - TP, AT_KV), F32)

    R = T * IDX_HEADS
    qi = (qi_ref[0] * np.float32(IDX_DIM ** -0.5)).astype(BF16)
    dots = jnp.maximum(_dot_nt(qi, kcat[...].astype(BF16)), 0.0)
    wsc = dots * (wi_ref[0] * np.float32(IDX_HEADS ** -0.5))
    sc = jnp.sum(wsc.reshape(T, IDX_HEADS, LP), axis=1)
    colpos = lax.broadcasted_iota(I32, (T, LP), 1)
    rowpos = past + lax.broadcasted_iota(I32, (T, LP), 0)
    causal = colpos <= rowpos
    key = jnp.where(causal, _sort_key(sc), INT_MIN)

    def count(mask):
        return jnp.sum(jnp.where(mask, 1.0, 0.0), axis=1, keepdims=True).astype(I32)

    thr = _kth_largest_key(lambda cand: count(key >= cand), T, topk)
    n_gt = count(key > thr)
    tie = key == thr
    lim = _tie_limit(lambda cand: count(tie & (colpos < cand)), topk - n_gt, T, LP)
    tie_bias = jnp.where(tie, jnp.where(colpos <= lim, 0.0, -jnp.inf), -jnp.inf)
    bias = jnp.where(causal, jnp.where(key > thr, 0.0, tie_bias), -jnp.inf)
    bias = jnp.broadcast_to(bias[:, None, :], (T, AT_HEADS, LP)).reshape(T * AT_HEADS, LP)

    qbd = (qbd_ref[0] * np.float32(AT_HEAD_DIM ** -0.5)).astype(BF16)
    s = _dot_nt(qbd, kc[...].astype(BF16)) + bias
    m = jnp.max(s, axis=-1, keepdims=True)
    pr = jnp.exp(s - m)
    den = jnp.sum(pr, axis=-1, keepdims=True)
    o_ref[0] = _dot(pr.astype(BF16), vc[...].astype(BF16)) / den


def dsa_sample(page_table, layer, qi_s, wi_s, q_bd, k_new, v_new, kw_new, pool_k, pool_v, pool_i, *, T):
    DB, n_pages = page_table.shape
    past = n_pages * PAGE_SIZE
    LP = past + PAGE_SIZE
    topk = min(TOPK_MAX, (past + T) // 4)
    per_b = lambda a: pl.BlockSpec((1,) + a.shape[1:], lambda b, pt, l: (b,) + (0,) * (a.ndim - 1))

    def page(width, jj):
        return pl.BlockSpec((None, None, PAGE_SIZE, width), lambda b, pt, l, jj=jj: (l[0], pt[b, jj], 0, 0))

    in_specs = [per_b(a) for a in (qi_s, wi_s, q_bd, k_new, v_new, kw_new)]
    in_specs += [page(AT_KV, jj) for jj in range(n_pages)]
    in_specs += [page(AT_KV, jj) for jj in range(n_pages)]
    in_specs += [page(IDX_DIM, jj) for jj in range(n_pages)]
    grid_spec = pltpu.PrefetchScalarGridSpec(
        num_scalar_prefetch=2,
        grid=(DB,),
        in_specs=in_specs,
        out_specs=pl.BlockSpec((1, T * AT_HEADS, AT_KV), lambda b, pt, l: (b, 0, 0)),
        scratch_shapes=[pltpu.VMEM((LP, IDX_DIM), F32), pltpu.VMEM((LP, AT_KV), F32), pltpu.VMEM((LP, AT_KV), F32)],
    )
    return pl.pallas_call(
        functools.partial(_dsa_sample_kernel, n_pages=n_pages, T=T, topk=topk),
        grid_spec=grid_spec,
        out_shape=jax.ShapeDtypeStruct((DB, T * AT_HEADS, AT_KV), F32),
        compiler_params=_params("arbitrary"),
        name="dsa_sample",
    )(page_table, layer, qi_s, wi_s, q_bd, k_new, v_new, kw_new,
      *([pool_k] * n_pages), *([pool_v] * n_pages), *([pool_i] * n_pages))


def _layer_norm(y, g, b):
    mu = jnp.mean(y, axis=-1, keepdims=True)
    yc = y - mu
    var = jnp.mean(yc * yc, axis=-1, keepdims=True)
    return yc * lax.rsqrt(var + LN_EPS) * g + b


def _group_allreduce(x, op, lane):
    d = 1
    while d < GROUP_SIZE:
        up = pltpu.roll(x, LANES - d, axis=1)
        dn = pltpu.roll(x, d, axis=1)
        x = op(x, jnp.where((lane & d) == 0, up, dn))
        d *= 2
    return x


def _mix_out_kernel(orw_ref, oat_ref, gate_ref, x_ref, wrw_ref, wat_ref, wout_ref, lng_ref, lnb_ref,
                    rw_ref, rb_ref, x1_ref, x1b_ref, rank_ref, gatew_ref, cnt_ref, *, alpha):
    d = x_ref.shape[1]
    tm = x_ref.shape[0]
    a = _dot(orw_ref[...].astype(BF16), wrw_ref[...])
    b = _dot(oat_ref[...].astype(BF16), wat_ref[...])
    merged = gate_ref[:, :d] * a + gate_ref[:, d:] * b
    y = alpha * x_ref[...] + _dot(merged.astype(BF16), wout_ref[...])
    x1 = _layer_norm(y, lng_ref[...], lnb_ref[...])
    x1_ref[...] = x1
    x1b_ref[...] = x1.astype(BF16)

    lane = lax.broadcasted_iota(I32, (tm, LANES), 1)
    lanef = lane.astype(F32)
    real = lane < N_EXPERTS
    ninf = -jnp.inf
    far = np.float32(4 * LANES)
    s = _sigmoid(_dot(x1, rw_ref[...], precision=HIGHEST))
    sb = jnp.where(real, s + rb_ref[...], ninf)
    m1 = _group_allreduce(sb, jnp.maximum, lane)
    first = _group_allreduce(jnp.where(sb == m1, lanef, far), jnp.minimum, lane)
    m2 = _group_allreduce(jnp.where(lanef == first, ninf, sb), jnp.maximum, lane)
    gs = jnp.where(real, m1 + m2, ninf)
    gid = (lane >> 3).astype(F32)
    picked = jnp.zeros((tm, LANES), F32)
    for _ in range(TOPK_GROUPS):
        mx = jnp.max(gs, axis=1, keepdims=True)
        gfirst = jnp.min(jnp.where(gs == mx, gid, far), axis=1, keepdims=True)
        hit = gid == gfirst
        picked = jnp.where(hit, 1.0, picked)
        gs = jnp.where(hit, ninf, gs)
    cand = jnp.where((picked > 0.0) & real, sb, ninf)
    sel = jnp.zeros((tm, LANES), F32)
    for _ in range(TOP_K):
        mx = jnp.max(cand, axis=1, keepdims=True)
        idx = jnp.min(jnp.where(cand == mx, lanef, far), axis=1, keepdims=True)
        hit = lanef == idx
        sel = jnp.where(hit, 1.0, sel)
        cand = jnp.where(hit, ninf, cand)
    gate_w = sel * s
    gate_w = gate_w / jnp.sum(gate_w, axis=1, keepdims=True) * np.float32(ROUTED_SCALE)
    ri = lax.broadcasted_iota(I32, (tm, tm), 0)
    ci = lax.broadcasted_iota(I32, (tm, tm), 1)
    before = jnp.where(ri > ci, 1.0, 0.0).astype(BF16)
    rank = jnp.where(sel > 0.0, _dot(before, sel.astype(BF16)), -1.0)
    rank_ref[0] = rank.T
    gatew_ref[0] = gate_w.T
    cnt_ref[0] = jnp.sum(sel, axis=0, keepdims=True)


def mix_out(o_rw, o_at, gate, x, w_rw, w_at, w_out, ln_g, ln_b, router_w, router_b, *, tm, alpha):
    n, d = x.shape
    nt = n // tm
    row = lambda w: pl.BlockSpec((tm, w), lambda i: (i, 0))
    full = lambda a: pl.BlockSpec(a.shape, lambda i: (0,) * a.ndim)
    tab = pl.BlockSpec((1, LANES, tm), lambda i: (i, 0, 0))
    return pl.pallas_call(
        functools.partial(_mix_out_kernel, alpha=alpha),
        grid=(nt,),
        in_specs=[row(RW_DIM), row(AT_Q), row(2 * d), row(d), full(w_rw), full(w_at), full(w_out),
                  full(ln_g), full(ln_b), full(router_w), full(router_b)],
        out_specs=[row(d), row(d), tab, tab, pl.BlockSpec((1, 1, LANES), lambda i: (i, 0, 0))],
        out_shape=[jax.ShapeDtypeStruct((n, d), F32), jax.ShapeDtypeStruct((n, d), BF16),
                   jax.ShapeDtypeStruct((nt, LANES, tm), F32), jax.ShapeDtypeStruct((nt, LANES, tm), F32),
                   jax.ShapeDtypeStruct((nt, 1, LANES), F32)],
        compiler_params=_params("parallel"),
        name="mix_out_route",
    )(o_rw, o_at, gate, x, w_rw, w_at, w_out, ln_g, ln_b, router_w, router_b)


CELL = 16
WIN = 64


RING = 8


def _expert_kernel(be_ref, i0_ref, i1_ref, nu_ref, cs_ref, cl_ref, ps_ref, pt_ref, np_ref, l_ref,
                   rank_ref, gatew_ref, x_hbm, w1_ref, w3_ref, w2_ref, o_ref,
                   xbuf, sem, xblk, gblk, *, BM, tm, nt):
    b = pl.program_id(0)
    n_used = nu_ref[0]
    n_pairs = np_ref[0]

    def tile_copy(tile, slot):
        return pltpu.make_async_copy(x_hbm.at[pl.ds(pl.multiple_of(tile * tm, tm), tm), :], xbuf.at[slot], sem.at[slot])

    @pl.when(b == 0)
    def _():
        for k in range(RING - 1):
            @pl.when(k < n_pairs)
            def _():
                tile_copy(pt_ref[k], k).start()

    @pl.when(b < n_used)
    def _():
        e = be_ref[b]
        base = b * BM
        first = i0_ref[b]
        xblk[...] = jnp.zeros(xblk.shape, BF16)
        gblk[...] = jnp.zeros(gblk.shape, F32)
        riota = lax.broadcasted_iota(I32, (WIN, tm), 0).astype(F32)

        def pair(i, _):
            p = ps_ref[b] + (i - first)
            slot = lax.rem(p, RING)
            ahead = p + (RING - 1)

            @pl.when(ahead < n_pairs)
            def _():
                tile_copy(pt_ref[ahead], lax.rem(ahead, RING)).start()

            tile_copy(i, slot).wait()
            off = cs_ref[e * nt + i] - base
            lo = jnp.maximum(off, 0)
            hi = jnp.minimum(off + cl_ref[e * nt + i], BM)
            rank_row = rank_ref[pl.ds(i, 1), :]
            gate_row = gatew_ref[pl.ds(i, 1), :]
            xt = xbuf[slot]

            def window(c, _):
                w = pl.multiple_of(lo + WIN * c, CELL)
                match = rank_row == riota + (w - off).astype(F32)
                onehot = jnp.where(match, 1.0, 0.0).astype(BF16)
                xblk[pl.ds(w, WIN), :] = _dot(onehot, xt).astype(BF16)
                gw = jnp.sum(jnp.where(match, gate_row, 0.0), axis=1, keepdims=True)
                gblk[pl.ds(w, WIN), :] = jnp.broadcast_to(gw, (WIN, LANES))
                return 0
            lax.fori_loop(0, jnp.maximum((hi - lo + WIN - 1) // WIN, 0), window, 0)
            return 0
        lax.fori_loop(first, i1_ref[b] + 1, pair, 0)

        x = xblk[0:BM, :]
        h1 = _dot(x, w1_ref[...])
        h3 = _dot(x, w3_ref[...])
        h = h1 * _sigmoid(h1) * h3
        y = _dot(h.astype(BF16), w2_ref[...])
        o_ref[...] = y * _lane_tile(gblk[0:BM, :], y.shape[1] // LANES)

    @pl.when(b >= n_used)
    def _():
        o_ref[...] = jnp.zeros(o_ref.shape, F32)


def expert_ffn(block_exp, i0, i1, n_used, cell_start, cell_len, pair_start, pair_tile, n_pairs, layer,
               rank_e, gatew_e, x1b, w1, w3, w2, *, BM, tm, n_blocks):
    d = x1b.shape[1]
    ff = w1.shape[-1]
    nt_pad = rank_e.shape[1]
    nt = x1b.shape[0] // tm
    sp = lambda f: (lambda b, be, i0, i1, nu, cs, cl, ps, pt, np_, l: f(b, be, l))
    grid_spec = pltpu.PrefetchScalarGridSpec(
        num_scalar_prefetch=10,
        grid=(n_blocks,),
        in_specs=[pl.BlockSpec((None, nt_pad, tm), sp(lambda b, be, l: (be[b], 0, 0))),
                  pl.BlockSpec((None, nt_pad, tm), sp(lambda b, be, l: (be[b], 0, 0))),
                  pl.BlockSpec(memory_space=pl.ANY),
                  pl.BlockSpec((None, None, d, ff), sp(lambda b, be, l: (l[0], be[b], 0, 0))),
                  pl.BlockSpec((None, None, d, ff), sp(lambda b, be, l: (l[0], be[b], 0, 0))),
                  pl.BlockSpec((None, None, ff, d), sp(lambda b, be, l: (l[0], be[b], 0, 0)))],
        out_specs=pl.BlockSpec((BM, d), sp(lambda b, be, l: (b, 0))),
        scratch_shapes=[pltpu.VMEM((RING, tm, d), BF16), pltpu.SemaphoreType.DMA((RING,)),
                        pltpu.VMEM((BM + WIN, d), BF16), pltpu.VMEM((BM + WIN, LANES), F32)],
    )
    return pl.pallas_call(
        functools.partial(_expert_kernel, BM=BM, tm=tm, nt=nt),
        grid_spec=grid_spec,
        out_shape=jax.ShapeDtypeStruct((n_blocks * BM, d), F32),
        compiler_params=_params("arbitrary"),
        name="expert_ffn",
    )(block_exp, i0, i1, n_used, cell_start, cell_len, pair_start, pair_tile, n_pairs, layer,
      rank_e, gatew_e, x1b, w1, w3, w2)


COMBINE_GROUP = 8


def _combine_kernel(cs_ref, nr_ref, rank_ref, x_ref, y_hbm, w1_ref, w3_ref, w2_ref, lng_ref, lnb_ref, o_ref,
                    ybuf, sem, acc, *, alpha):
    i = pl.program_id(0)
    tm = x_ref.shape[0]
    n_groups = N_EXPERTS // COMBINE_GROUP
    riota = lax.broadcasted_iota(I32, (WIN, tm), 0).astype(F32)
    acc[...] = jnp.zeros(acc.shape, F32)

    def one_round(c, _):
        def window_copy(g, k, slot):
            start = pl.multiple_of(cs_ref[i * N_EXPERTS + g * COMBINE_GROUP + k] + WIN * c, CELL)
            return pltpu.make_async_copy(y_hbm.at[pl.ds(start, WIN), :], ybuf.at[slot, k * WIN:(k + 1) * WIN, :],
                                         sem.at[slot])

        for k in range(COMBINE_GROUP):
            window_copy(0, k, 0).start()
        for g in range(n_groups):
            slot = g % 2
            if g + 1 < n_groups:
                for k in range(COMBINE_GROUP):
                    window_copy(g + 1, k, 1 - slot).start()
            for k in range(COMBINE_GROUP):
                window_copy(g, k, slot).wait()
            y = ybuf[slot]
            y_hi = y.astype(BF16)
            y_lo = (y - y_hi.astype(F32)).astype(BF16)
            target = riota + (WIN * c).astype(F32)
            onehot = jnp.concatenate(
                [jnp.where(rank_ref[0, g * COMBINE_GROUP + k:g * COMBINE_GROUP + k + 1, :] == target, 1.0, 0.0)
                 for k in range(COMBINE_GROUP)], axis=0).astype(BF16)
            acc[...] += _dot_tn(onehot, y_hi) + _dot_tn(onehot, y_lo)
        return 0
    lax.fori_loop(0, nr_ref[i], one_round, 0)

    x = x_ref[...]
    xb = x.astype(BF16)
    h1 = _dot(xb, w1_ref[...])
    h3 = _dot(xb, w3_ref[...])
    sh = _dot((h1 * _sigmoid(h1) * h3).astype(BF16), w2_ref[...])
    o_ref[...] = _layer_norm(alpha * x + (acc[...] + sh), lng_ref[...], lnb_ref[...])


def combine_shared_ln(cell_start_t, n_rounds, rank_t, x, yg, w1, w3, w2, ln_g, ln_b, *, tm, alpha):
    n, d = x.shape
    sp = lambda f: (lambda i, cs, nr: f(i))
    full = lambda a: pl.BlockSpec(a.shape, sp(lambda i: (0,) * a.ndim))
    row = pl.BlockSpec((tm, d), sp(lambda i: (i, 0)))
    grid_spec = pltpu.PrefetchScalarGridSpec(
        num_scalar_prefetch=2,
        grid=(n // tm,),
        in_specs=[pl.BlockSpec((1, LANES, tm), sp(lambda i: (i, 0, 0))), row, pl.BlockSpec(memory_space=pl.ANY),
                  full(w1), full(w3), full(w2), full(ln_g), full(ln_b)],
        out_specs=row,
        scratch_shapes=[pltpu.VMEM((2, COMBINE_GROUP * WIN, d), F32), pltpu.SemaphoreType.DMA((2,)),
                        pltpu.VMEM((tm, d), F32)],
    )
    return pl.pallas_call(
        functools.partial(_combine_kernel, alpha=alpha),
        grid_spec=grid_spec,
        out_shape=jax.ShapeDtypeStruct((n, d), F32),
        compiler_params=_params("arbitrary"),
        name="combine_shared_ln2",
    )(cell_start_t, n_rounds, rank_t, x, yg, w1, w3, w2, ln_g, ln_b)


def moe(x1, x1b, rank_t, gatew_t, cnt_t, layer, w1, w3, w2, s1, s3, s2, ln_g, ln_b, *, BM, tm, alpha):
    n, d = x1.shape
    nt = n // tm
    E = N_EXPERTS
    cnt = cnt_t[:, 0, :E].astype(I32)
    clen = (cnt + CELL - 1) // CELL * CELL
    region = (jnp.sum(clen, axis=0) + BM - 1) // BM * BM
    reg_end = jnp.cumsum(region)
    cstart = (reg_end - region)[None, :] + jnp.cumsum(clen, axis=0) - clen
    n_blocks = -(-(n * TOP_K + (CELL - 1) * nt * E + (BM - 1) * E) // BM) + 1
    blk0 = jnp.arange(n_blocks, dtype=I32) * BM
    block_exp = jnp.minimum(jnp.sum(reg_end[None, :] <= blk0[:, None], axis=1), E - 1).astype(I32)
    n_used = (reg_end[-1:] // BM).astype(I32)
    cs_b = cstart.T[block_exp]
    i0 = jnp.sum(cs_b + clen.T[block_exp] <= blk0[:, None], axis=1).astype(I32)
    i1 = (jnp.sum(cs_b < blk0[:, None] + BM, axis=1) - 1).astype(I32)
    n_rounds = jnp.maximum((jnp.max(clen, axis=1) + WIN - 1) // WIN, 0).astype(I32)
    per_block = jnp.where(jnp.arange(n_blocks) < n_used[0], i1 - i0 + 1, 0).astype(I32)
    pair_end = jnp.cumsum(per_block)
    pair_start = (pair_end - per_block).astype(I32)
    pid = jnp.arange(n_blocks + nt * E, dtype=I32)
    pblk = jnp.minimum(jnp.sum(pair_end[None, :] <= pid[:, None], axis=1), n_blocks - 1)
    pair_tile = jnp.clip(i0[pblk] + pid - pair_start[pblk], 0, nt - 1).astype(I32)
    nt_pad = -(-nt // SUBLANES) * SUBLANES
    e_major = lambda a: jnp.pad(jnp.transpose(a, (1, 0, 2))[:E], ((0, 0), (0, nt_pad - nt), (0, 0)))
    yg = expert_ffn(block_exp, i0, i1, n_used, cstart.T.reshape(-1).astype(I32), clen.T.reshape(-1).astype(I32),
                    pair_start, pair_tile, pair_end[-1:].astype(I32), layer,
                    e_major(rank_t), e_major(gatew_t), x1b, w1, w3, w2, BM=BM, tm=tm, n_blocks=n_blocks)
    return combine_shared_ln(cstart.reshape(-1).astype(I32), n_rounds, rank_t, x1, yg, s1, s3, s2, ln_g, ln_b,
                             tm=tm, alpha=alpha)


def kernel(x_prompt, x_sample, cache_k, cache_v, cache_idx, state_rwkv, state_shift, page_table, w_in, b_gate, rw_mu, rw_w2, rw_w0, rw_a2, rw_a0, rw_g2, rw_kk, rw_ka, rw_rk, rw_gn_g, rw_gn_b, rw_proj, at_proj, w_out, ln1_g, ln1_b, router_w, router_b, ex_w1, ex_w3, ex_w2, sh_w1, sh_w3, sh_w2, ln2_g, ln2_b):
    B, S, d = x_prompt.shape
    DB, T, _ = x_sample.shape
    depth = w_in.shape[0]
    n_pages = page_table.shape[1]
    past = n_pages * PAGE_SIZE
    n_pool = cache_k.shape[1]
    NP = B * S
    NS = DB * T
    N = NP + NS
    alpha = float((2 * depth) ** 0.25)
    TM = 256
    QB = 256
    CH = 64
    GP = 2 if B % 2 == 0 else 1
    GS = 2 if DB % 2 == 0 else 1
    TP = SUBLANES * (-(-T // SUBLANES))
    BM = 256
    assert NP % TM == 0 and NS % TM == 0 and S % TM == 0 and S % QB == 0 and S % CH == 0

    split = np.cumsum([RW_COLS, AT_Q, AT_KV, AT_KV, IDX_HEADS * IDX_DIM, IDX_DIM, IDX_HEADS])
    w_rw, w_q, w_k, w_v, w_qi, w_ki, w_wi, w_g = jnp.split(w_in, split.tolist(), axis=-1)
    w_all = jnp.concatenate(
        [w_rw, w_q, w_k, w_v, w_qi, w_ki, w_wi,
         jnp.zeros((depth, d, LANES - IDX_DIM - IDX_HEADS), F32), w_g], axis=-1).astype(BF16)
    zlr = jnp.zeros((depth, RW_DECAY_RANK, RW_DIM), F32)
    wwa = jnp.concatenate([jnp.concatenate([rw_w2, zlr], -1), jnp.concatenate([zlr, rw_a2], -1)], 1).astype(BF16)
    g2 = rw_g2.astype(BF16)
    vecs = jnp.stack([rw_w0, rw_a0, rw_kk, rw_ka, rw_rk.reshape(depth, RW_DIM), rw_gn_g, rw_gn_b,
                      jnp.zeros_like(rw_w0)], axis=1)
    w_rwp, w_atp, w_o = rw_proj.astype(BF16), at_proj.astype(BF16), w_out.astype(BF16)
    r_w = jnp.concatenate([router_w, jnp.zeros((depth, d, LANES - N_EXPERTS), F32)], -1)
    r_b = jnp.concatenate([router_b, jnp.zeros((depth, LANES - N_EXPERTS), F32)], -1)[:, None, :]
    e1, e3, e2 = ex_w1.astype(BF16), ex_w3.astype(BF16), ex_w2.astype(BF16)
    s1, s3, s2 = sh_w1.astype(BF16), sh_w3.astype(BF16), sh_w2.astype(BF16)
    pool_k = cache_k.reshape(depth, n_pool, PAGE_SIZE, AT_KV)
    pool_v = cache_v.reshape(depth, n_pool, PAGE_SIZE, AT_KV)
    kv_eye = jnp.repeat(jnp.eye(AT_KV_HEADS, dtype=F32), AT_GROUP, axis=0)

    cos_t, sin_t = rope_table(S, past, T, TM)
    zero_state = jnp.zeros((B, RW_HEADS, RW_HEAD_DIM, RW_HEAD_DIM), F32)
    zero_row = jnp.zeros((B, 1, RW_COLS), F32)

    x = jnp.concatenate([x_prompt.reshape(NP, d), x_sample.reshape(NS, d)], axis=0)
    outs = {k: [] for k in ("kp", "vp", "ip", "rp", "sp", "ks", "vs", "is", "rs", "ss")}
    for l in range(depth):
        layer = jnp.full((1,), l, I32)
        outs["sp"].append(x[:NP].reshape(B, S, d)[:, -1])
        outs["ss"].append(x[NP:].reshape(DB, T, d)[:, -1])
        p_rw, q, k, v, qi, kw, gate, k_bf, v_bf, kw_bf = project(
            x, w_all[l], b_gate[l][None, :], cos_t, sin_t, tm=TM, n_prompt=NP, seq=S)
        outs["kp"].append(k[:NP].reshape(B, S, AT_KV_HEADS, AT_HEAD_DIM))
        outs["vp"].append(v[:NP].reshape(B, S, AT_KV_HEADS, AT_HEAD_DIM))
        outs["ip"].append(kw[:NP, :IDX_DIM].reshape(B, S, IDX_DIM))
        outs["ks"].append(k[NP:].reshape(DB, T, AT_KV_HEADS, AT_HEAD_DIM))
        outs["vs"].append(v[NP:].reshape(DB, T, AT_KV_HEADS, AT_HEAD_DIM))
        outs["is"].append(kw[NP:, :IDX_DIM].reshape(DB, T, IDX_DIM))

        rw_args = (rw_mu[l][None, :], wwa[l], g2[l], vecs[l])
        pp = p_rw[:NP].reshape(B, S, RW_COLS)
        pp_prev = jnp.concatenate([zero_row, pp[:, :-1]], axis=1)
        o_rw_p, st_p = rwkv7(pp, pp_prev, zero_state, *rw_args, G=GP, C=CH, valid=CH)
        p0 = shift_project(state_shift[l], w_all[l])
        ps = p_rw[NP:].reshape(DB, T, RW_COLS)
        ps_prev = jnp.concatenate([p0[:, None], ps[:, :-1]], axis=1)
        padt = ((0, 0), (0, TP - T), (0, 0))
        o_rw_s, st_s = rwkv7(jnp.pad(ps, padt), jnp.pad(ps_prev, padt), state_rwkv[l], *rw_args, G=GS, C=TP, valid=T)
        outs["rp"].append(st_p)
        outs["rs"].append(st_s)
        o_rw = jnp.concatenate([o_rw_p.reshape(NP, RW_DIM), o_rw_s[:, :T].reshape(NS, RW_DIM)], axis=0)

        seq3 = lambda a: a[:NP].reshape(B, S, a.shape[-1])
        q_bd_p = (q[:NP].reshape(NP, AT_HEADS, 1, AT_HEAD_DIM) * (kv_eye * np.float32(AT_HEAD_DIM ** -0.5))[None, :, :, None]
                  ).astype(BF16).reshape(B, S, AT_HEADS * AT_KV)
        qi_pad = jnp.pad(qi[:NP].reshape(NP, IDX_HEADS, IDX_DIM) * np.float32(IDX_DIM ** -0.5),
                         ((0, 0), (0, 0), (0, LANES - IDX_DIM))).astype(BF16).reshape(B, S, IDX_HEADS * LANES)
        o_at_p = dsa_prompt(q_bd_p, qi_pad, seq3(kw), seq3(k_bf), seq3(v_bf), seq3(kw_bf), QB=QB)
        q_s = q[NP:].reshape(DB, T, AT_HEADS, 1, AT_HEAD_DIM)
        q_bd = (q_s * kv_eye[None, None, :, :, None]).reshape(DB, T * AT_HEADS, AT_KV)
        qi_s = qi[NP:].reshape(DB, T * IDX_HEADS, IDX_DIM)
        wi_s = kw[NP:, IDX_DIM:IDX_DIM + IDX_HEADS].reshape(DB, T * IDX_HEADS, 1)
        new = lambda a: jnp.pad(a[NP:].reshape(DB, T, a.shape[-1]), padt)
        o_bd = dsa_sample(page_table, layer, qi_s, wi_s, q_bd, new(k), new(v), new(kw),
                          pool_k, pool_v, cache_idx, T=T)
        o_at_s = jnp.sum(o_bd.reshape(DB, T, AT_HEADS, AT_KV_HEADS, AT_HEAD_DIM) * kv_eye[None, None, :, :, None], axis=3)
        o_at = jnp.concatenate([o_at_p.reshape(NP, AT_Q), o_at_s.reshape(NS, AT_Q)], axis=0)

        x1, x1b, rank_t, gatew_t, cnt_t = mix_out(o_rw, o_at, gate, x, w_rwp[l], w_atp[l], w_o[l], ln1_g[l][None, :],
                                                  ln1_b[l][None, :], r_w[l], r_b[l], tm=TM, alpha=alpha)
        x = moe(x1, x1b, rank_t, gatew_t, cnt_t, layer, e1, e3, e2, s1[l], s3[l], s2[l],
                ln2_g[l][None, :], ln2_b[l][None, :], BM=BM, tm=TM, alpha=alpha)

    st = lambda name: jnp.stack(outs[name])
    return (x[:NP].reshape(B, S, d), x[NP:].reshape(DB, T, d),
            st("kp"), st("vp"), st("ip"), st("rp"), st("sp"),
            st("ks"), st("vs"), st("is"), st("rs"), st("ss"))
```

```python
import functools

import numpy as np
import jax
import jax.numpy as jnp
from jax import lax
from jax.experimental import pallas as pl
from jax.experimental.pallas import tpu as pltpu

F32 = jnp.float32
BF16 = jnp.bfloat16
I32 = jnp.int32
HIGHEST = lax.Precision.HIGHEST

RW_HEADS = 8
RW_HEAD_DIM = 64
RW_DIM = RW_HEADS * RW_HEAD_DIM
RW_DECAY_RANK = 64
RW_AAA_RANK = 64
RW_GATE_RANK = 128
RW_GN_EPS = 64e-5
RW_COLS = 3 * RW_DIM + RW_DECAY_RANK + RW_AAA_RANK + RW_GATE_RANK

AT_HEADS = 8
AT_KV_HEADS = 4
AT_HEAD_DIM = 64
AT_GROUP = AT_HEADS // AT_KV_HEADS
AT_Q = AT_HEADS * AT_HEAD_DIM
AT_KV = AT_KV_HEADS * AT_HEAD_DIM
IDX_HEADS = 8
IDX_DIM = 64
TOPK_MAX = 256
PAGE_SIZE = 128
ROPE_THETA = 10000.0

N_EXPERTS = 64
N_GROUPS = 8
GROUP_SIZE = N_EXPERTS // N_GROUPS
TOPK_GROUPS = 4
TOP_K = 8
EXPERT_FF = 256
SHARED_FF = 256
ROUTED_SCALE = 2.5
LN_EPS = 1e-5

LANES = 128
SUBLANES = 8
VMEM_LIMIT = 56 * 1024 * 1024

C_RW = 0
C_Q = C_RW + RW_COLS
C_K = C_Q + AT_Q
C_V = C_K + AT_KV
C_QI = C_V + AT_KV
C_KW = C_QI + IDX_HEADS * IDX_DIM
C_GATE = C_KW + LANES
INT_MIN = np.int32(-2 ** 31)


def _dot(a, b, **kw):
    return jnp.dot(a, b, preferred_element_type=F32, **kw)


def _dot_nt(a, b):
    return lax.dot_general(a, b, (((1,), (1,)), ((), ())), preferred_element_type=F32)


def _dot_tn(a, b):
    return lax.dot_general(a, b, (((0,), (0,)), ((), ())), preferred_element_type=F32)


def _params(*sem):
    return pltpu.CompilerParams(dimension_semantics=sem, vmem_limit_bytes=VMEM_LIMIT)


def _sigmoid(x):
    return 1.0 / (1.0 + jnp.exp(-x))


def _lane_tile(x, reps):
    return jnp.concatenate([x] * reps, axis=1) if reps > 1 else x


def _sort_key(s):
    s = jnp.where(s == 0.0, 0.0, s)
    bits = lax.bitcast_convert_type(s, I32)
    return bits ^ ((bits >> 31) & np.int32(0x7FFFFFFF))


def _rope_table_kernel(inv_ref, cos_ref, sin_ref, *, tm, seq, past, dec_seq):
    i = pl.program_id(0)
    row = i * tm + lax.broadcasted_iota(I32, (tm, LANES), 0)
    pos = jnp.where(row < seq, row, past + lax.rem(jnp.maximum(row - seq, 0), dec_seq))
    ang = pos.astype(F32) * inv_ref[...]
    lane = lax.broadcasted_iota(I32, (tm, LANES), 1)
    s = jnp.sin(ang)
    cos_ref[...] = jnp.cos(ang)
    sin_ref[...] = jnp.where((lane & (AT_HEAD_DIM - 1)) < AT_HEAD_DIM // 2, -s, s)


def rope_table(seq, past, dec_seq, tm):
    half = AT_HEAD_DIM // 2
    inv = (np.float32(ROPE_THETA) ** (-np.arange(half, dtype=np.float32) * np.float32(2.0 / AT_HEAD_DIM))).astype(np.float32)
    inv_row = jnp.asarray(np.tile(inv, LANES // half)[None, :])
    rows = seq + tm
    out = jax.ShapeDtypeStruct((rows, LANES), F32)
    return pl.pallas_call(
        functools.partial(_rope_table_kernel, tm=tm, seq=seq, past=past, dec_seq=dec_seq),
        grid=(rows // tm,),
        in_specs=[pl.BlockSpec((1, LANES), lambda i: (0, 0))],
        out_specs=[pl.BlockSpec((tm, LANES), lambda i: (i, 0))] * 2,
        out_shape=[out, out],
        compiler_params=_params("parallel"),
        name="rope_table",
    )(inv_row)


def _rope(x, cos, sin_signed, lo):
    up = pltpu.roll(x, LANES - AT_HEAD_DIM // 2, axis=1)
    dn = pltpu.roll(x, AT_HEAD_DIM // 2, axis=1)
    return x * cos + jnp.where(lo, up, dn) * sin_signed


def _proj_kernel(x_ref, w_ref, bg_ref, cos_ref, sin_ref,
                 prw_ref, q_ref, k_ref, v_ref, qi_ref, kw_ref, gate_ref, kbf_ref, vbf_ref, kwbf_ref, qbd_ref, qip_ref):
    xb = x_ref[...].astype(BF16)
    tm = xb.shape[0]
    cos = cos_ref[...]
    sin = sin_ref[...]
    lane = lax.broadcasted_iota(I32, (tm, LANES), 1)
    lo = (lane & (AT_HEAD_DIM - 1)) < AT_HEAD_DIM // 2
    first = lane < AT_HEAD_DIM
    zeros_bf = jnp.zeros((tm, LANES), BF16)

    prw_ref[...] = _dot(xb, w_ref[:, C_RW:C_Q])

    def roped(c0, width, out_ref, bf_ref=None, per_tile=None):
        p = _dot(xb, w_ref[:, c0:c0 + width])
        for g in range(width // LANES):
            sl = slice(g * LANES, (g + 1) * LANES)
            y = _rope(p[:, sl], cos, sin, lo)
            out_ref[:, sl] = y
            if bf_ref is not None:
                bf_ref[:, sl] = y.astype(BF16)
            if per_tile is not None:
                per_tile(g, y)

    def q_block_diag(g, y):
        ys = y * np.float32(AT_HEAD_DIM ** -0.5)
        swapped = pltpu.roll(ys, AT_HEAD_DIM, axis=1)
        for s in range(2):
            h = 2 * g + s
            kv = h // AT_GROUP
            dst_first = kv % 2 == 0
            src = ys if (s == 0) == dst_first else swapped
            piece = jnp.where(first if dst_first else jnp.logical_not(first), src, 0.0).astype(BF16)
            for t in range(AT_KV // LANES):
                qbd_ref[:, h * AT_KV + t * LANES:h * AT_KV + (t + 1) * LANES] = piece if t == kv // 2 else zeros_bf

    def qi_padded(g, y):
        ys = y * np.float32(IDX_DIM ** -0.5)
        swapped = pltpu.roll(ys, IDX_DIM, axis=1)
        for s in range(2):
            h = 2 * g + s
            qip_ref[:, h * LANES:(h + 1) * LANES] = jnp.where(first, ys if s == 0 else swapped, 0.0).astype(BF16)

    roped(C_Q, AT_Q, q_ref, per_tile=q_block_diag)
    roped(C_K, AT_KV, k_ref, kbf_ref)
    roped(C_QI, IDX_HEADS * IDX_DIM, qi_ref, per_tile=qi_padded)
    v = _dot(xb, w_ref[:, C_V:C_QI])
    v_ref[...] = v
    vbf_ref[...] = v.astype(BF16)
    kw = _dot(xb, w_ref[:, C_KW:C_GATE])
    kw = jnp.where(lane < IDX_DIM, _rope(kw, cos, sin, lo), kw)
    kw_ref[...] = kw
    kwbf_ref[...] = kw.astype(BF16)
    gate_ref[...] = _sigmoid(_dot(xb, w_ref[:, C_GATE:]) + bg_ref[...])


def project(x, w_all, b_gate, cos_t, sin_t, *, tm, n_prompt, seq):
    n, d = x.shape
    n_tab = seq // tm
    npt = n_prompt // tm

    def tab_map(i):
        return (jnp.where(i < npt, lax.rem(i, n_tab), n_tab), 0)

    row = lambda w: pl.BlockSpec((tm, w), lambda i: (i, 0))
    outs = [(RW_COLS, F32), (AT_Q, F32), (AT_KV, F32), (AT_KV, F32), (IDX_HEADS * IDX_DIM, F32), (LANES, F32),
            (2 * d, F32), (AT_KV, BF16), (AT_KV, BF16), (LANES, BF16),
            (AT_HEADS * AT_KV, BF16), (IDX_HEADS * LANES, BF16)]
    return pl.pallas_call(
        _proj_kernel,
        grid=(n // tm,),
        in_specs=[row(d),
                  pl.BlockSpec(w_all.shape, lambda i: (0, 0)),
                  pl.BlockSpec((1, 2 * d), lambda i: (0, 0)),
                  pl.BlockSpec((tm, LANES), tab_map),
                  pl.BlockSpec((tm, LANES), tab_map)],
        out_specs=[row(w) for w, _ in outs],
        out_shape=[jax.ShapeDtypeStruct((n, w), dt) for w, dt in outs],
        compiler_params=_params("parallel"),
        name="in_proj",
    )(x, w_all, b_gate, cos_t, sin_t)


def _mm_kernel(x_ref, w_ref, o_ref):
    o_ref[...] = _dot(x_ref[...].astype(BF16), w_ref[...])


def shift_project(rows, w_all):
    m, d = rows.shape
    return pl.pallas_call(
        _mm_kernel,
        grid=(1,),
        in_specs=[pl.BlockSpec((m, d), lambda i: (0, 0)), pl.BlockSpec((d, RW_COLS), lambda i: (0, 0))],
        out_specs=pl.BlockSpec((m, RW_COLS), lambda i: (0, 0)),
        out_shape=jax.ShapeDtypeStruct((m, RW_COLS), F32),
        compiler_params=_params("arbitrary"),
        name="shift_proj",
    )(rows, w_all)


def _wkv_kernel(p_ref, pp_ref, s0_ref, mu_ref, wwa_ref, g2_ref, vec_ref, o_ref, sfin_ref, s_scr, *, G, C, valid):
    j = pl.program_id(1)
    D = RW_DIM
    N = RW_HEAD_DIM
    R = G * C

    @pl.when(j == 0)
    def _():
        s_scr[...] = s0_ref[...].reshape(G * RW_HEADS, N, N)

    p = p_ref[...].reshape(R, RW_COLS)
    xs = p + mu_ref[...] * (pp_ref[...].reshape(R, RW_COLS) - p)
    r = xs[:, 0:D]
    k = xs[:, D:2 * D]
    v = xs[:, 2 * D:3 * D]
    x_lr = xs[:, 3 * D:3 * D + LANES]
    lane = lax.broadcasted_iota(I32, (R, LANES), 1)
    lr_in = jnp.where(lane < RW_DECAY_RANK, jnp.tanh(x_lr), x_lr)
    lr = _dot(lr_in.astype(BF16), wwa_ref[...])
    w0, a0, kkw, ka, rk, gng, gnb = (vec_ref[i:i + 1, :] for i in range(7))
    z = -(w0 + lr[:, :D])
    softplus = jnp.maximum(z, 0.0) + jnp.log1p(jnp.exp(-jnp.abs(z)))
    logw = -jnp.exp(-softplus - 0.5)
    if valid < C:
        rowid = lax.broadcasted_iota(I32, (R, D), 0)
        logw = jnp.where(lax.rem(rowid, C) < valid, logw, 0.0)
    a_sig = _sigmoid(a0 + lr[:, D:])
    g = _dot(_sigmoid(xs[:, 3 * D + LANES:]).astype(BF16), g2_ref[...])
    kkv = k * kkw
    k2 = k * (1.0 + (a_sig - 1.0) * ka)
    bonus_in = r * k2 * rk

    rr = lax.broadcasted_iota(I32, (R, R), 0)
    cr = lax.broadcasted_iota(I32, (R, R), 1)
    same_seq = (rr >= cr) & (rr - lax.rem(rr, C) <= cr)
    cl = _dot(jnp.where(same_seq, 1.0, 0.0), logw, precision=HIGHEST)
    e_pos = jnp.exp(cl)
    e_neg = jnp.exp(-cl)
    e_prev = jnp.exp(cl - logw)
    rt = r * e_pos
    kt = k2 * e_neg
    ri = lax.broadcasted_iota(I32, (C, C), 0)
    ci = lax.broadcasted_iota(I32, (C, C), 1)
    strict = ri > ci
    incl = ri >= ci

    units = [(q, h) for q in range(G) for h in range(RW_HEADS)]
    rows = lambda q: slice(q * C, (q + 1) * C)
    lanes = lambda h: slice(h * N, (h + 1) * N)
    cut = lambda a, q, h: a[rows(q), lanes(h)]
    bf = lambda a: a.astype(BF16)
    kk = []
    for q, h in units:
        kq = cut(kkv, q, h)
        kk.append(kq / jnp.maximum(jnp.sqrt(jnp.sum(kq * kq, axis=-1, keepdims=True)), 1e-12))
    ar = [bf(jnp.concatenate([-kk[n] * cut(e_prev, q, h), cut(rt, q, h)], axis=0)) for n, (q, h) in enumerate(units)]
    bk = [bf(jnp.concatenate([kk[n] * cut(a_sig, q, h) * cut(e_neg, q, h), cut(kt, q, h)], axis=0))
          for n, (q, h) in enumerate(units)]
    vh = [cut(v, q, h) for q, h in units]
    s_old = [s_scr[n] for n in range(len(units))]
    pm = [_dot_nt(ar[n], bk[n]) for n in range(len(units))]
    x0 = [_dot_nt(ar[n], bf(s_old[n])) for n in range(len(units))]
    a_ab = [jnp.where(strict, m[:C, :C], 0.0) for m in pm]
    a_rb = [bf(jnp.where(incl, m[C:, :C], 0.0)) for m in pm]
    a_kk = [bf(jnp.concatenate([jnp.where(strict, m[:C, C:], 0.0), jnp.where(incl, m[C:, C:], 0.0)], axis=0)) for m in pm]
    av = [_dot(a_kk[n], bf(vh[n])) for n in range(len(units))]
    nn = a_ab
    lp = a_ab
    for _ in range(max(C.bit_length() - 2, 0)):
        lpb = [bf(m) for m in lp]
        lp = [_dot(m, m) for m in lpb]
        nn = [a + b + _dot(bf(a), bf(b)) for a, b in zip(nn, lp)]
    rhs = [x[:C] + a[:C] for x, a in zip(x0, av)]
    u = [a + _dot(bf(m), bf(a)) for a, m in zip(rhs, nn)]
    y = [x0[n][C:] + av[n][C:] + _dot(a_rb[n], bf(u[n])) for n in range(len(units))]
    ds = [_dot_tn(bf(jnp.concatenate([u[n], vh[n]], axis=0)), bk[n]) for n in range(len(units))]
    for n, (q, h) in enumerate(units):
        last = (q + 1) * C - 1
        s_scr[n] = (s_old[n] + ds[n]) * e_pos[last:last + 1, lanes(h)]
        ym = jnp.mean(y[n], axis=-1, keepdims=True)
        yc = y[n] - ym
        yv = jnp.mean(yc * yc, axis=-1, keepdims=True)
        yn = yc * lax.rsqrt(yv + RW_GN_EPS) * gng[:, lanes(h)] + gnb[:, lanes(h)]
        bonus = jnp.sum(cut(bonus_in, q, h), axis=-1, keepdims=True) * vh[n]
        o_ref[q, :, lanes(h)] = (yn + bonus) * cut(g, q, h)

    @pl.when(j == pl.num_programs(1) - 1)
    def _():
        sfin_ref[...] = s_scr[...].reshape(G, RW_HEADS, N, N)


def rwkv7(p, p_prev, s0, mu, wwa, g2, vecs, *, G, C, valid):
    nseq, T, _ = p.shape
    tok = pl.BlockSpec((G, C, RW_COLS), lambda b, j: (b, j, 0))
    st = pl.BlockSpec((G, RW_HEADS, RW_HEAD_DIM, RW_HEAD_DIM), lambda b, j: (b, 0, 0, 0))
    full = lambda a: pl.BlockSpec(a.shape, lambda b, j: (0,) * a.ndim)
    return pl.pallas_call(
        functools.partial(_wkv_kernel, G=G, C=C, valid=valid),
        grid=(nseq // G, T // C),
        in_specs=[tok, tok, st, full(mu), full(wwa), full(g2), full(vecs)],
        out_specs=[pl.BlockSpec((G, C, RW_DIM), lambda b, j: (b, j, 0)), st],
        out_shape=[jax.ShapeDtypeStruct((nseq, T, RW_DIM), F32), jax.ShapeDtypeStruct(s0.shape, F32)],
        scratch_shapes=[pltpu.VMEM((G * RW_HEADS, RW_HEAD_DIM, RW_HEAD_DIM), F32)],
        compiler_params=_params("parallel", "arbitrary"),
        name="rwkv7_chunk%d" % C,
    )(p, p_prev, s0, mu, wwa, g2, vecs)


RADIX_BITS = 4


def _radix_search(accept, start, nbits, rows, radix_bits=RADIX_BITS):
    steps = -(-nbits // radix_bits)

    def body(i, t):
        unit = jnp.left_shift(np.int32(1), radix_bits * (steps - 1 - i))
        digit = jnp.zeros((rows, 1), I32)
        for k in range(1, 2 ** radix_bits):
            digit = digit + jnp.where(accept(t + unit * k), 1, 0)
        return t + unit * digit
    return lax.fori_loop(0, steps, body, start)


def _kth_largest_key(count_ge, rows, topk):
    return _radix_search(lambda cand: count_ge(cand) >= topk, jnp.full((rows, 1), INT_MIN, I32), 32, rows)


def _tie_limit(count_tie_below, need, rows, ncols, radix_bits=RADIX_BITS):
    nbits = max((ncols - 1).bit_length(), 1)
    return _radix_search(lambda cand: count_tie_below(cand) < need, jnp.zeros((rows, 1), I32), nbits, rows, radix_bits)


RSUB = 128


def _dsa_prompt_kernel(qbd_ref, qip_ref, kwq_ref, kbf_ref, vbf_ref, kibf_ref, o_ref,
                       key_scr, a_scr, b_scr, jl_scr, w_scr, m_scr, l_scr, acc_scr, *, QB, topk, S):
    j = pl.program_id(1)
    nkb = j + 1
    row0 = j * QB + lax.broadcasted_iota(I32, (RSUB, QB), 0)
    col0 = lax.broadcasted_iota(I32, (RSUB, QB), 1)
    reps = QB // LANES

    def cols(kb):
        return pl.ds(pl.multiple_of(kb * QB, QB), QB)

    kwq = kwq_ref[...]
    for h in range(IDX_HEADS):
        w_scr[h] = jnp.broadcast_to(kwq[:, IDX_DIM + h:IDX_DIM + h + 1] * np.float32(IDX_HEADS ** -0.5), (QB, LANES))

    def score_block(kb, _):
        ki = kibf_ref[cols(kb), :]
        for r0 in range(0, QB, RSUB):
            rs = slice(r0, r0 + RSUB)
            dots = [_dot_nt(qip_ref[rs, h * LANES:(h + 1) * LANES], ki) for h in range(IDX_HEADS)]
            sc = jnp.zeros((RSUB, QB), F32)
            for h in range(IDX_HEADS):
                sc = sc + jnp.maximum(dots[h], 0.0) * _lane_tile(w_scr[h, rs, :], reps)
            causal = kb * QB + col0 <= row0 + r0
            sc = jnp.where(sc == 0.0, 0.0, sc)
            key_scr[rs, cols(kb)] = jnp.where(causal, _sort_key(sc), INT_MIN)
            a_scr[rs, cols(kb)] = jnp.where(causal, sc, -jnp.inf).astype(BF16)
        return 0
    lax.fori_loop(0, nkb, score_block, 0)

    ones_kb = jnp.ones((QB, LANES), BF16)
    one_bf = jnp.ones((), BF16)
    zero_bf = jnp.zeros((), BF16)
    kf = np.float32(topk)

    def count_packed(src, cand, strict=False):
        cand_t = _lane_tile(cand.astype(BF16), reps)

        def blk(kb, acc):
            x = src[:, cols(kb)]
            hit = (x > cand_t) if strict else (x >= cand_t)
            return acc + jnp.where(hit, one_bf, zero_bf)
        hits = lax.fori_loop(0, nkb, blk, jnp.zeros((QB, QB), BF16))
        return _dot(hits, ones_kb)

    key16_neg_inf = np.int32(-32641)

    def key16_value(c):
        bits = c ^ ((c >> 15) & np.int32(0x7FFF))
        val = lax.bitcast_convert_type(jnp.left_shift(bits, 16), F32)
        return jnp.where(c <= key16_neg_inf, -jnp.inf, val)

    def stage_a(i, t):
        c = t + jnp.left_shift(np.int32(1), 15 - i)
        return jnp.where(count_packed(a_scr, key16_value(c)) >= kf, c, t)
    t16 = lax.fori_loop(0, 16, stage_a, jnp.full((QB, LANES), -32768, I32))
    thr_a = key16_value(t16)
    open_row = t16 <= key16_neg_inf
    need = kf - count_packed(a_scr, thr_a, strict=True)
    key_c = _sort_key(thr_a)
    key_c_t = _lane_tile(key_c, reps)
    thr_a_t = _lane_tile(thr_a, reps)

    def band_offset(kb):
        band = a_scr[:, cols(kb)].astype(F32) == thr_a_t
        return band, key_scr[:, cols(kb)] - key_c_t + np.int32(32768)

    def prep_hi(kb, _):
        band, d = band_offset(kb)
        b_scr[:, cols(kb)] = jnp.where(band, (d >> 8).astype(F32), -1.0).astype(BF16)
        return 0
    lax.fori_loop(0, nkb, prep_hi, 0)

    def stage_hi(i, t):
        c = t + jnp.left_shift(np.int32(1), 8 - i)
        ok = (count_packed(b_scr, c.astype(F32)) >= need) & (c <= 256)
        return jnp.where(ok, c, t)
    t_hi = lax.fori_loop(0, 9, stage_hi, jnp.zeros((QB, LANES), I32))
    need = need - count_packed(b_scr, t_hi.astype(F32), strict=True)
    t_hi_t = _lane_tile(t_hi, reps)

    def prep_lo(kb, _):
        band, d = band_offset(kb)
        b_scr[:, cols(kb)] = jnp.where(band & ((d >> 8) == t_hi_t), (d & 255).astype(F32), -1.0).astype(BF16)
        return 0
    lax.fori_loop(0, nkb, prep_lo, 0)

    def stage_lo(i, t):
        c = t + jnp.left_shift(np.int32(1), 7 - i)
        return jnp.where(count_packed(b_scr, c.astype(F32)) >= need, c, t)
    t_lo = lax.fori_loop(0, 8, stage_lo, jnp.zeros((QB, LANES), I32))
    n_gt_lo = count_packed(b_scr, t_lo.astype(F32), strict=True)
    n_ties = count_packed(b_scr, t_lo.astype(F32)) - n_gt_lo
    slots = need - n_gt_lo
    thr_l = jnp.where(open_row, INT_MIN, key_c - np.int32(32768) + jnp.left_shift(t_hi, 8) + t_lo)
    thr_t = _lane_tile(thr_l, reps)
    excess = (n_ties > slots) & jnp.logical_not(open_row)
    jl_scr[...] = jnp.full((QB, 1), S, I32)

    @pl.when(jnp.max(jnp.where(excess, 1.0, 0.0)) > 0.5)
    def _():
        thr1 = thr_l[:, :1]
        lane0 = lax.broadcasted_iota(I32, (QB, LANES), 1)

        def count_tie_below(cand):
            def blk(kb, acc):
                kblk = key_scr[:, cols(kb)]
                for c in range(reps):
                    hit = (kblk[:, c * LANES:(c + 1) * LANES] == thr1) & (kb * QB + c * LANES + lane0 < cand)
                    acc = acc + jnp.where(hit, 1.0, 0.0)
                return acc
            acc = lax.fori_loop(0, nkb, blk, jnp.zeros((QB, LANES), F32))
            return jnp.sum(acc, axis=1, keepdims=True).astype(I32)
        lim = _tie_limit(count_tie_below, slots[:, :1].astype(I32), QB, S, radix_bits=1)
        jl_scr[...] = jnp.where(excess[:, :1], lim, S)

    jlim = jl_scr[...]

    m_scr[...] = jnp.full(m_scr.shape, -jnp.inf, F32)
    l_scr[...] = jnp.zeros(l_scr.shape, F32)
    acc_scr[...] = jnp.zeros(acc_scr.shape, F32)

    def attend_block(kb, _):
        colpos = kb * QB + col0
        kk = kbf_ref[cols(kb), :]
        vv = vbf_ref[cols(kb), :]
        for r0 in range(0, QB, RSUB):
            rs = slice(r0, r0 + RSUB)
            kblk = key_scr[rs, cols(kb)]
            tie_bias = jnp.where(kblk == thr_t[rs], jnp.where(colpos <= jl_scr[rs, :], 0.0, -jnp.inf), -jnp.inf)
            bias = jnp.where(colpos <= row0 + r0, jnp.where(kblk > thr_t[rs], 0.0, tie_bias), -jnp.inf)
            heads = range(AT_HEADS)
            s = [_dot_nt(qbd_ref[rs, h * AT_KV:(h + 1) * AT_KV], kk) + bias for h in heads]
            m_old = [m_scr[h, rs, :] for h in heads]
            m_new = [jnp.maximum(m_old[h], jnp.max(s[h], axis=1, keepdims=True)) for h in heads]
            m_use = [jnp.where(m == -jnp.inf, 0.0, m) for m in m_new]
            pr = [jnp.exp(s[h] - _lane_tile(m_use[h], reps)) for h in heads]
            alpha = [jnp.exp(m_old[h] - m_use[h]) for h in heads]
            pv = [_dot(pr[h].astype(BF16), vv[:, ((h // AT_GROUP) // 2) * LANES:((h // AT_GROUP) // 2 + 1) * LANES])
                  for h in heads]
            for h in heads:
                l_scr[h, rs, :] = alpha[h] * l_scr[h, rs, :] + jnp.sum(pr[h], axis=1, keepdims=True)
                acc_scr[h, rs, :] = alpha[h] * acc_scr[h, rs, :] + pv[h]
                m_scr[h, rs, :] = m_new[h]
        return 0
    lax.fori_loop(0, nkb, attend_block, 0)

    for h in range(AT_HEADS):
        off = ((h // AT_GROUP) % 2) * AT_HEAD_DIM
        o_ref[:, h * AT_HEAD_DIM:(h + 1) * AT_HEAD_DIM] = (acc_scr[h] / l_scr[h])[:, off:off + AT_HEAD_DIM]


def dsa_prompt(q_bd, qi_pad, kw, k_bf, v_bf, kw_bf, *, B, S, QB):
    topk = min(TOPK_MAX, S // 4)
    nq = S // QB
    qblk = lambda w: pl.BlockSpec((QB, w), lambda b, j: (b * nq + j, 0))
    seq = lambda w: pl.BlockSpec((S, w), lambda b, j: (b, 0))
    hstate = pltpu.VMEM((AT_HEADS, QB, LANES), F32)
    return pl.pallas_call(
        functools.partial(_dsa_prompt_kernel, QB=QB, topk=topk, S=S),
        grid=(B, nq),
        in_specs=[qblk(AT_HEADS * AT_KV), qblk(IDX_HEADS * LANES), qblk(LANES), seq(AT_KV), seq(AT_KV), seq(LANES)],
        out_specs=qblk(AT_Q),
        out_shape=jax.ShapeDtypeStruct((B * S, AT_Q), F32),
        scratch_shapes=[pltpu.VMEM((QB, S), I32), pltpu.VMEM((QB, S), BF16), pltpu.VMEM((QB, S), BF16),
                        pltpu.VMEM((QB, 1), I32), pltpu.VMEM((IDX_HEADS, QB, LANES), F32), hstate, hstate, hstate],
        compiler_params=_params("parallel", "arbitrary"),
        name="dsa_prompt",
    )(q_bd, qi_pad, kw, k_bf, v_bf, kw_bf)


def _dsa_sample_kernel(pt_ref, l_ref, qi_ref, wi_ref, qbd_ref, kn_ref, vn_ref, kin_ref, *rest, n_pages, T, topk):
    kp = rest[:n_pages]
    vp = rest[n_pages:2 * n_pages]
    ip = rest[2 * n_pages:3 * n_pages]
    o_ref, kcat, kc, vc = rest[3 * n_pages:]
    past = n_pages * PAGE_SIZE
    LP = past + PAGE_SIZE
    TP = kn_ref.shape[1]
    for jj in range(n_pages):
        rows = slice(jj * PAGE_SIZE, (jj + 1) * PAGE_SIZE)
        kcat[rows, :] = ip[jj][...]
        kc[rows, :] = kp[jj][...]
        vc[rows, :] = vp[jj][...]
    kcat[past:past + TP, :] = kin_ref[0][:, :IDX_DIM]
    kc[past:past + TP, :] = kn_ref[0]
    vc[past:past + TP, :] = vn_ref[0]
    kcat[past + TP:, :] = jnp.zeros((LP - past - TP, IDX_DIM), F32)
    kc[past + TP:, :] = jnp.zeros((LP - past - TP, AT_KV), F32)
    vc[past + TP:, :] = jnp.zeros((LP - past - TP, AT_KV), F32)

    R = T * IDX_HEADS
    qi = (qi_ref[0] * np.float32(IDX_DIM ** -0.5)).astype(BF16)
    dots = jnp.maximum(_dot_nt(qi, kcat[...].astype(BF16)), 0.0)
    wsc = dots * (wi_ref[0] * np.float32(IDX_HEADS ** -0.5))
    sc = jnp.sum(wsc.reshape(T, IDX_HEADS, LP), axis=1)
    colpos = lax.broadcasted_iota(I32, (T, LP), 1)
    rowpos = past + lax.broadcasted_iota(I32, (T, LP), 0)
    causal = colpos <= rowpos
    key = jnp.where(causal, _sort_key(sc), INT_MIN)

    def count(mask):
        return jnp.sum(jnp.where(mask, 1.0, 0.0), axis=1, keepdims=True).astype(I32)

    thr = _kth_largest_key(lambda cand: count(key >= cand), T, topk)
    n_gt = count(key > thr)
    tie = key == thr
    lim = _tie_limit(lambda cand: count(tie & (colpos < cand)), topk - n_gt, T, LP)
    tie_bias = jnp.where(tie, jnp.where(colpos <= lim, 0.0, -jnp.inf), -jnp.inf)
    bias = jnp.where(causal, jnp.where(key > thr, 0.0, tie_bias), -jnp.inf)
    bias = jnp.broadcast_to(bias[:, None, :], (T, AT_HEADS, LP)).reshape(T * AT_HEADS, LP)

    qbd = (qbd_ref[0] * np.float32(AT_HEAD_DIM ** -0.5)).astype(BF16)
    s = _dot_nt(qbd, kc[...].astype(BF16)) + bias
    m = jnp.max(s, axis=-1, keepdims=True)
    pr = jnp.exp(s - m)
    den = jnp.sum(pr, axis=-1, keepdims=True)
    o_ref[0] = _dot(pr.astype(BF16), vc[...].astype(BF16)) / den


def dsa_sample(page_table, layer, qi_s, wi_s, q_bd, k_new, v_new, kw_new, pool_k, pool_v, pool_i, *, T):
    DB, n_pages = page_table.shape
    past = n_pages * PAGE_SIZE
    LP = past + PAGE_SIZE
    topk = min(TOPK_MAX, (past + T) // 4)
    per_b = lambda a: pl.BlockSpec((1,) + a.shape[1:], lambda b, pt, l: (b,) + (0,) * (a.ndim - 1))

    def page(width, jj):
        return pl.BlockSpec((None, None, PAGE_SIZE, width), lambda b, pt, l, jj=jj: (l[0], pt[b, jj], 0, 0))

    in_specs = [per_b(a) for a in (qi_s, wi_s, q_bd, k_new, v_new, kw_new)]
    in_specs += [page(AT_KV, jj) for jj in range(n_pages)]
    in_specs += [page(AT_KV, jj) for jj in range(n_pages)]
    in_specs += [page(IDX_DIM, jj) for jj in range(n_pages)]
    grid_spec = pltpu.PrefetchScalarGridSpec(
        num_scalar_prefetch=2,
        grid=(DB,),
        in_specs=in_specs,
        out_specs=pl.BlockSpec((1, T * AT_HEADS, AT_KV), lambda b, pt, l: (b, 0, 0)),
        scratch_shapes=[pltpu.VMEM((LP, IDX_DIM), F32), pltpu.VMEM((LP, AT_KV), F32), pltpu.VMEM((LP, AT_KV), F32)],
    )
    return pl.pallas_call(
        functools.partial(_dsa_sample_kernel, n_pages=n_pages, T=T, topk=topk),
        grid_spec=grid_spec,
        out_shape=jax.ShapeDtypeStruct((DB, T * AT_HEADS, AT_KV), F32),
        compiler_params=_params("arbitrary"),
        name="dsa_sample",
    )(page_table, layer, qi_s, wi_s, q_bd, k_new, v_new, kw_new,
      *([pool_k] * n_pages), *([pool_v] * n_pages), *([pool_i] * n_pages))


def _layer_norm(y, g, b):
    mu = jnp.mean(y, axis=-1, keepdims=True)
    yc = y - mu
    var = jnp.mean(yc * yc, axis=-1, keepdims=True)
    return yc * lax.rsqrt(var + LN_EPS) * g + b


def _group_allreduce(x, op, lane):
    d = 1
    while d < GROUP_SIZE:
        up = pltpu.roll(x, LANES - d, axis=1)
        dn = pltpu.roll(x, d, axis=1)
        x = op(x, jnp.where((lane & d) == 0, up, dn))
        d *= 2
    return x


def _mix_out_kernel(orw_ref, oat_ref, gate_ref, x_ref, wrw_ref, wat_ref, wout_ref, lng_ref, lnb_ref,
                    rw_ref, rb_ref, x1_ref, x1b_ref, rank_ref, gatew_ref, cnt_ref, *, alpha):
    d = x_ref.shape[1]
    tm = x_ref.shape[0]
    a = _dot(orw_ref[...].astype(BF16), wrw_ref[...])
    b = _dot(oat_ref[...].astype(BF16), wat_ref[...])
    merged = gate_ref[:, :d] * a + gate_ref[:, d:] * b
    y = alpha * x_ref[...] + _dot(merged.astype(BF16), wout_ref[...])
    x1 = _layer_norm(y, lng_ref[...], lnb_ref[...])
    x1_ref[...] = x1
    x1b_ref[...] = x1.astype(BF16)

    lane = lax.broadcasted_iota(I32, (tm, LANES), 1)
    lanef = lane.astype(F32)
    real = lane < N_EXPERTS
    ninf = -jnp.inf
    far = np.float32(4 * LANES)
    s = _sigmoid(_dot(x1, rw_ref[...], precision=HIGHEST))
    sb = jnp.where(real, s + rb_ref[...], ninf)
    m1 = _group_allreduce(sb, jnp.maximum, lane)
    first = _group_allreduce(jnp.where(sb == m1, lanef, far), jnp.minimum, lane)
    m2 = _group_allreduce(jnp.where(lanef == first, ninf, sb), jnp.maximum, lane)
    gs = jnp.where(real, m1 + m2, ninf)
    gid = (lane >> 3).astype(F32)
    picked = jnp.zeros((tm, LANES), F32)
    for _ in range(TOPK_GROUPS):
        mx = jnp.max(gs, axis=1, keepdims=True)
        gfirst = jnp.min(jnp.where(gs == mx, gid, far), axis=1, keepdims=True)
        hit = gid == gfirst
        picked = jnp.where(hit, 1.0, picked)
        gs = jnp.where(hit, ninf, gs)
    cand = jnp.where((picked > 0.0) & real, sb, ninf)
    sel = jnp.zeros((tm, LANES), F32)
    for _ in range(TOP_K):
        mx = jnp.max(cand, axis=1, keepdims=True)
        idx = jnp.min(jnp.where(cand == mx, lanef, far), axis=1, keepdims=True)
        hit = lanef == idx
        sel = jnp.where(hit, 1.0, sel)
        cand = jnp.where(hit, ninf, cand)
    gate_w = sel * s
    gate_w = gate_w / jnp.sum(gate_w, axis=1, keepdims=True) * np.float32(ROUTED_SCALE)
    ri = lax.broadcasted_iota(I32, (tm, tm), 0)
    ci = lax.broadcasted_iota(I32, (tm, tm), 1)
    before = jnp.where(ri > ci, 1.0, 0.0).astype(BF16)
    rank = jnp.where(sel > 0.0, _dot(before, sel.astype(BF16)), -1.0)
    rank_ref[0] = rank.T
    gatew_ref[0] = gate_w.T
    cnt_ref[0] = jnp.sum(sel, axis=0, keepdims=True)


def mix_out(o_rw, o_at, gate, x, w_rw, w_at, w_out, ln_g, ln_b, router_w, router_b, *, tm, alpha):
    n, d = x.shape
    nt = n // tm
    row = lambda w: pl.BlockSpec((tm, w), lambda i: (i, 0))
    full = lambda a: pl.BlockSpec(a.shape, lambda i: (0,) * a.ndim)
    tab = pl.BlockSpec((1, LANES, tm), lambda i: (i, 0, 0))
    return pl.pallas_call(
        functools.partial(_mix_out_kernel, alpha=alpha),
        grid=(nt,),
        in_specs=[row(RW_DIM), row(AT_Q), row(2 * d), row(d), full(w_rw), full(w_at), full(w_out),
                  full(ln_g), full(ln_b), full(router_w), full(router_b)],
        out_specs=[row(d), row(d), tab, tab, pl.BlockSpec((1, 1, LANES), lambda i: (i, 0, 0))],
        out_shape=[jax.ShapeDtypeStruct((n, d), F32), jax.ShapeDtypeStruct((n, d), BF16),
                   jax.ShapeDtypeStruct((nt, LANES, tm), F32), jax.ShapeDtypeStruct((nt, LANES, tm), F32),
                   jax.ShapeDtypeStruct((nt, 1, LANES), F32)],
        compiler_params=_params("parallel"),
        name="mix_out_route",
    )(o_rw, o_at, gate, x, w_rw, w_at, w_out, ln_g, ln_b, router_w, router_b)


CELL = 16
WIN = 64


RING = 8


def _expert_kernel(be_ref, i0_ref, i1_ref, nu_ref, cs_ref, cl_ref, ps_ref, pt_ref, np_ref, l_ref,
                   rank_ref, gatew_ref, x_hbm, w1_ref, w3_ref, w2_ref, o_ref,
                   xbuf, sem, xblk, gblk, *, BM, tm, nt):
    b = pl.program_id(0)
    n_used = nu_ref[0]
    n_pairs = np_ref[0]

    def tile_copy(tile, slot):
        return pltpu.make_async_copy(x_hbm.at[pl.ds(pl.multiple_of(tile * tm, tm), tm), :], xbuf.at[slot], sem.at[slot])

    @pl.when(b == 0)
    def _():
        for k in range(RING - 1):
            @pl.when(k < n_pairs)
            def _():
                tile_copy(pt_ref[k], k).start()

    @pl.when(b < n_used)
    def _():
        e = be_ref[b]
        base = b * BM
        first = i0_ref[b]
        xblk[...] = jnp.zeros(xblk.shape, BF16)
        gblk[...] = jnp.zeros(gblk.shape, F32)
        riota = lax.broadcasted_iota(I32, (WIN, tm), 0).astype(F32)

        def pair(i, _):
            p = ps_ref[b] + (i - first)
            slot = lax.rem(p, RING)
            ahead = p + (RING - 1)

            @pl.when(ahead < n_pairs)
            def _():
                tile_copy(pt_ref[ahead], lax.rem(ahead, RING)).start()

            tile_copy(i, slot).wait()
            off = cs_ref[e * nt + i] - base
            lo = jnp.maximum(off, 0)
            hi = jnp.minimum(off + cl_ref[e * nt + i], BM)
            rank_row = rank_ref[pl.ds(i, 1), :]
            gate_row = gatew_ref[pl.ds(i, 1), :]
            xt = xbuf[slot]

            def window(c, _):
                w = pl.multiple_of(lo + WIN * c, CELL)
                match = rank_row == riota + (w - off).astype(F32)
                onehot = jnp.where(match, 1.0, 0.0).astype(BF16)
                xblk[pl.ds(w, WIN), :] = _dot(onehot, xt).astype(BF16)
                gw = jnp.sum(jnp.where(match, gate_row, 0.0), axis=1, keepdims=True)
                gblk[pl.ds(w, WIN), :] = jnp.broadcast_to(gw, (WIN, LANES))
                return 0
            lax.fori_loop(0, jnp.maximum((hi - lo + WIN - 1) // WIN, 0), window, 0)
            return 0
        lax.fori_loop(first, i1_ref[b] + 1, pair, 0)

        x = xblk[0:BM, :]
        h1 = _dot(x, w1_ref[...])
        h3 = _dot(x, w3_ref[...])
        h = h1 * _sigmoid(h1) * h3
        y = _dot(h.astype(BF16), w2_ref[...])
        o_ref[...] = y * _lane_tile(gblk[0:BM, :], y.shape[1] // LANES)

    @pl.when(b >= n_used)
    def _():
        o_ref[...] = jnp.zeros(o_ref.shape, F32)


def expert_ffn(block_exp, i0, i1, n_used, cell_start, cell_len, pair_start, pair_tile, n_pairs, layer,
               rank_e, gatew_e, x1b, w1, w3, w2, *, BM, tm, n_blocks):
    d = x1b.shape[1]
    ff = w1.shape[-1]
    nt_pad = rank_e.shape[1]
    nt = x1b.shape[0] // tm
    sp = lambda f: (lambda b, be, i0, i1, nu, cs, cl, ps, pt, np_, l: f(b, be, l))
    grid_spec = pltpu.PrefetchScalarGridSpec(
        num_scalar_prefetch=10,
        grid=(n_blocks,),
        in_specs=[pl.BlockSpec((None, nt_pad, tm), sp(lambda b, be, l: (be[b], 0, 0))),
                  pl.BlockSpec((None, nt_pad, tm), sp(lambda b, be, l: (be[b], 0, 0))),
                  pl.BlockSpec(memory_space=pl.ANY),
                  pl.BlockSpec((None, None, d, ff), sp(lambda b, be, l: (l[0], be[b], 0, 0))),
                  pl.BlockSpec((None, None, d, ff), sp(lambda b, be, l: (l[0], be[b], 0, 0))),
                  pl.BlockSpec((None, None, ff, d), sp(lambda b, be, l: (l[0], be[b], 0, 0)))],
        out_specs=pl.BlockSpec((BM, d), sp(lambda b, be, l: (b, 0))),
        scratch_shapes=[pltpu.VMEM((RING, tm, d), BF16), pltpu.SemaphoreType.DMA((RING,)),
                        pltpu.VMEM((BM + WIN, d), BF16), pltpu.VMEM((BM + WIN, LANES), F32)],
    )
    return pl.pallas_call(
        functools.partial(_expert_kernel, BM=BM, tm=tm, nt=nt),
        grid_spec=grid_spec,
        out_shape=jax.ShapeDtypeStruct((n_blocks * BM, d), F32),
        compiler_params=_params("arbitrary"),
        name="expert_ffn",
    )(block_exp, i0, i1, n_used, cell_start, cell_len, pair_start, pair_tile, n_pairs, layer,
      rank_e, gatew_e, x1b, w1, w3, w2)


COMBINE_GROUP = 8


def _combine_kernel(cs_ref, nr_ref, rank_ref, x_ref, y_hbm, w1_ref, w3_ref, w2_ref, lng_ref, lnb_ref, o_ref,
                    ybuf, sem, acc, *, alpha):
    i = pl.program_id(0)
    tm = x_ref.shape[0]
    n_groups = N_EXPERTS // COMBINE_GROUP
    riota = lax.broadcasted_iota(I32, (WIN, tm), 0).astype(F32)
    acc[...] = jnp.zeros(acc.shape, F32)

    def one_round(c, _):
        def window_copy(g, k, slot):
            start = pl.multiple_of(cs_ref[i * N_EXPERTS + g * COMBINE_GROUP + k] + WIN * c, CELL)
            return pltpu.make_async_copy(y_hbm.at[pl.ds(start, WIN), :], ybuf.at[slot, k * WIN:(k + 1) * WIN, :],
                                         sem.at[slot])

        for k in range(COMBINE_GROUP):
            window_copy(0, k, 0).start()
        for g in range(n_groups):
            slot = g % 2
            if g + 1 < n_groups:
                for k in range(COMBINE_GROUP):
                    window_copy(g + 1, k, 1 - slot).start()
            for k in range(COMBINE_GROUP):
                window_copy(g, k, slot).wait()
            y = ybuf[slot]
            y_hi = y.astype(BF16)
            y_lo = (y - y_hi.astype(F32)).astype(BF16)
            target = riota + (WIN * c).astype(F32)
            onehot = jnp.concatenate(
                [jnp.where(rank_ref[0, g * COMBINE_GROUP + k:g * COMBINE_GROUP + k + 1, :] == target, 1.0, 0.0)
                 for k in range(COMBINE_GROUP)], axis=0).astype(BF16)
            acc[...] += _dot_tn(onehot, y_hi) + _dot_tn(onehot, y_lo)
        return 0
    lax.fori_loop(0, nr_ref[i], one_round, 0)

    x = x_ref[...]
    xb = x.astype(BF16)
    h1 = _dot(xb, w1_ref[...])
    h3 = _dot(xb, w3_ref[...])
    sh = _dot((h1 * _sigmoid(h1) * h3).astype(BF16), w2_ref[...])
    o_ref[...] = _layer_norm(alpha * x + (acc[...] + sh), lng_ref[...], lnb_ref[...])


def combine_shared_ln(cell_start_t, n_rounds, rank_t, x, yg, w1, w3, w2, ln_g, ln_b, *, tm, alpha):
    n, d = x.shape
    sp = lambda f: (lambda i, cs, nr: f(i))
    full = lambda a: pl.BlockSpec(a.shape, sp(lambda i: (0,) * a.ndim))
    row = pl.BlockSpec((tm, d), sp(lambda i: (i, 0)))
    grid_spec = pltpu.PrefetchScalarGridSpec(
        num_scalar_prefetch=2,
        grid=(n // tm,),
        in_specs=[pl.BlockSpec((1, LANES, tm), sp(lambda i: (i, 0, 0))), row, pl.BlockSpec(memory_space=pl.ANY),
                  full(w1), full(w3), full(w2), full(ln_g), full(ln_b)],
        out_specs=row,
        scratch_shapes=[pltpu.VMEM((2, COMBINE_GROUP * WIN, d), F32), pltpu.SemaphoreType.DMA((2,)),
                        pltpu.VMEM((tm, d), F32)],
    )
    return pl.pallas_call(
        functools.partial(_combine_kernel, alpha=alpha),
        grid_spec=grid_spec,
        out_shape=jax.ShapeDtypeStruct((n, d), F32),
        compiler_params=_params("arbitrary"),
        name="combine_shared_ln2",
    )(cell_start_t, n_rounds, rank_t, x, yg, w1, w3, w2, ln_g, ln_b)


def moe(x1, x1b, rank_t, gatew_t, cnt_t, layer, w1, w3, w2, s1, s3, s2, ln_g, ln_b, *, BM, tm, alpha):
    n, d = x1.shape
    nt = n // tm
    E = N_EXPERTS
    cnt = cnt_t[:, 0, :E].astype(I32)
    clen = (cnt + CELL - 1) // CELL * CELL
    region = (jnp.sum(clen, axis=0) + BM - 1) // BM * BM
    reg_end = jnp.cumsum(region)
    cstart = (reg_end - region)[None, :] + jnp.cumsum(clen, axis=0) - clen
    n_blocks = -(-(n * TOP_K + (CELL - 1) * nt * E + (BM - 1) * E) // BM) + 1
    blk0 = jnp.arange(n_blocks, dtype=I32) * BM
    block_exp = jnp.minimum(jnp.sum(reg_end[None, :] <= blk0[:, None], axis=1), E - 1).astype(I32)
    n_used = (reg_end[-1:] // BM).astype(I32)
    cs_b = cstart.T[block_exp]
    i0 = jnp.sum(cs_b + clen.T[block_exp] <= blk0[:, None], axis=1).astype(I32)
    i1 = (jnp.sum(cs_b < blk0[:, None] + BM, axis=1) - 1).astype(I32)
    n_rounds = jnp.maximum((jnp.max(clen, axis=1) + WIN - 1) // WIN, 0).astype(I32)
    per_block = jnp.where(jnp.arange(n_blocks) < n_used[0], i1 - i0 + 1, 0).astype(I32)
    pair_end = jnp.cumsum(per_block)
    pair_start = (pair_end - per_block).astype(I32)
    pid = jnp.arange(n_blocks + nt * E, dtype=I32)
    pblk = jnp.minimum(jnp.sum(pair_end[None, :] <= pid[:, None], axis=1), n_blocks - 1)
    pair_tile = jnp.clip(i0[pblk] + pid - pair_start[pblk], 0, nt - 1).astype(I32)
    nt_pad = -(-nt // SUBLANES) * SUBLANES
    e_major = lambda a: jnp.pad(jnp.transpose(a, (1, 0, 2))[:E], ((0, 0), (0, nt_pad - nt), (0, 0)))
    yg = expert_ffn(block_exp, i0, i1, n_used, cstart.T.reshape(-1).astype(I32), clen.T.reshape(-1).astype(I32),
                    pair_start, pair_tile, pair_end[-1:].astype(I32), layer,
                    e_major(rank_t), e_major(gatew_t), x1b, w1, w3, w2, BM=BM, tm=tm, n_blocks=n_blocks)
    return combine_shared_ln(cstart.reshape(-1).astype(I32), n_rounds, rank_t, x1, yg, s1, s3, s2, ln_g, ln_b,
                             tm=tm, alpha=alpha)


def kernel(x_prompt, x_sample, cache_k, cache_v, cache_idx, state_rwkv, state_shift, page_table, w_in, b_gate, rw_mu, rw_w2, rw_w0, rw_a2, rw_a0, rw_g2, rw_kk, rw_ka, rw_rk, rw_gn_g, rw_gn_b, rw_proj, at_proj, w_out, ln1_g, ln1_b, router_w, router_b, ex_w1, ex_w3, ex_w2, sh_w1, sh_w3, sh_w2, ln2_g, ln2_b):
    B, S, d = x_prompt.shape
    DB, T, _ = x_sample.shape
    depth = w_in.shape[0]
    n_pages = page_table.shape[1]
    past = n_pages * PAGE_SIZE
    n_pool = cache_k.shape[1]
    NP = B * S
    NS = DB * T
    N = NP + NS
    alpha = float((2 * depth) ** 0.25)
    TM = 256
    QB = 256
    CH = 64
    GP = 2 if B % 2 == 0 else 1
    GS = 2 if DB % 2 == 0 else 1
    TP = SUBLANES * (-(-T // SUBLANES))
    BM = 256
    assert NP % TM == 0 and NS % TM == 0 and S % TM == 0 and S % QB == 0 and S % CH == 0

    split = np.cumsum([RW_COLS, AT_Q, AT_KV, AT_KV, IDX_HEADS * IDX_DIM, IDX_DIM, IDX_HEADS])
    w_rw, w_q, w_k, w_v, w_qi, w_ki, w_wi, w_g = jnp.split(w_in, split.tolist(), axis=-1)
    w_all = jnp.concatenate(
        [w_rw, w_q, w_k, w_v, w_qi, w_ki, w_wi,
         jnp.zeros((depth, d, LANES - IDX_DIM - IDX_HEADS), F32), w_g], axis=-1).astype(BF16)
    zlr = jnp.zeros((depth, RW_DECAY_RANK, RW_DIM), F32)
    wwa = jnp.concatenate([jnp.concatenate([rw_w2, zlr], -1), jnp.concatenate([zlr, rw_a2], -1)], 1).astype(BF16)
    g2 = rw_g2.astype(BF16)
    vecs = jnp.stack([rw_w0, rw_a0, rw_kk, rw_ka, rw_rk.reshape(depth, RW_DIM), rw_gn_g, rw_gn_b,
                      jnp.zeros_like(rw_w0)], axis=1)
    w_rwp, w_atp, w_o = rw_proj.astype(BF16), at_proj.astype(BF16), w_out.astype(BF16)
    r_w = jnp.concatenate([router_w, jnp.zeros((depth, d, LANES - N_EXPERTS), F32)], -1)
    r_b = jnp.concatenate([router_b, jnp.zeros((depth, LANES - N_EXPERTS), F32)], -1)[:, None, :]
    e1, e3, e2 = ex_w1.astype(BF16), ex_w3.astype(BF16), ex_w2.astype(BF16)
    s1, s3, s2 = sh_w1.astype(BF16), sh_w3.astype(BF16), sh_w2.astype(BF16)
    pool_k = cache_k.reshape(depth, n_pool, PAGE_SIZE, AT_KV)
    pool_v = cache_v.reshape(depth, n_pool, PAGE_SIZE, AT_KV)
    kv_eye = jnp.repeat(jnp.eye(AT_KV_HEADS, dtype=F32), AT_GROUP, axis=0)

    cos_t, sin_t = rope_table(S, past, T, TM)
    zero_state = jnp.zeros((B, RW_HEADS, RW_HEAD_DIM, RW_HEAD_DIM), F32)
    zero_row = jnp.zeros((B, 1, RW_COLS), F32)

    x = jnp.concatenate([x_prompt.reshape(NP, d), x_sample.reshape(NS, d)], axis=0)
    outs = {k: [] for k in ("kp", "vp", "ip", "rp", "sp", "ks", "vs", "is", "rs", "ss")}
    for l in range(depth):
        layer = jnp.full((1,), l, I32)
        outs["sp"].append(x[:NP].reshape(B, S, d)[:, -1])
        outs["ss"].append(x[NP:].reshape(DB, T, d)[:, -1])
        p_rw, q, k, v, qi, kw, gate, k_bf, v_bf, kw_bf, q_bd_all, qi_pad = project(
            x, w_all[l], b_gate[l][None, :], cos_t, sin_t, tm=TM, n_prompt=NP, seq=S)
        outs["kp"].append(k[:NP].reshape(B, S, AT_KV_HEADS, AT_HEAD_DIM))
        outs["vp"].append(v[:NP].reshape(B, S, AT_KV_HEADS, AT_HEAD_DIM))
        outs["ip"].append(kw[:NP, :IDX_DIM].reshape(B, S, IDX_DIM))
        outs["ks"].append(k[NP:].reshape(DB, T, AT_KV_HEADS, AT_HEAD_DIM))
        outs["vs"].append(v[NP:].reshape(DB, T, AT_KV_HEADS, AT_HEAD_DIM))
        outs["is"].append(kw[NP:, :IDX_DIM].reshape(DB, T, IDX_DIM))

        rw_args = (rw_mu[l][None, :], wwa[l], g2[l], vecs[l])
        pp = p_rw[:NP].reshape(B, S, RW_COLS)
        pp_prev = jnp.concatenate([zero_row, pp[:, :-1]], axis=1)
        o_rw_p, st_p = rwkv7(pp, pp_prev, zero_state, *rw_args, G=GP, C=CH, valid=CH)
        p0 = shift_project(state_shift[l], w_all[l])
        ps = p_rw[NP:].reshape(DB, T, RW_COLS)
        ps_prev = jnp.concatenate([p0[:, None], ps[:, :-1]], axis=1)
        padt = ((0, 0), (0, TP - T), (0, 0))
        o_rw_s, st_s = rwkv7(jnp.pad(ps, padt), jnp.pad(ps_prev, padt), state_rwkv[l], *rw_args, G=GS, C=TP, valid=T)
        outs["rp"].append(st_p)
        outs["rs"].append(st_s)
        o_rw = jnp.concatenate([o_rw_p.reshape(NP, RW_DIM), o_rw_s[:, :T].reshape(NS, RW_DIM)], axis=0)

        o_at_p = dsa_prompt(q_bd_all, qi_pad, kw, k_bf, v_bf, kw_bf, B=B, S=S, QB=QB)
        q_s = q[NP:].reshape(DB, T, AT_HEADS, 1, AT_HEAD_DIM)
        q_bd = (q_s * kv_eye[None, None, :, :, None]).reshape(DB, T * AT_HEADS, AT_KV)
        qi_s = qi[NP:].reshape(DB, T * IDX_HEADS, IDX_DIM)
        wi_s = kw[NP:, IDX_DIM:IDX_DIM + IDX_HEADS].reshape(DB, T * IDX_HEADS, 1)
        new = lambda a: jnp.pad(a[NP:].reshape(DB, T, a.shape[-1]), padt)
        o_bd = dsa_sample(page_table, layer, qi_s, wi_s, q_bd, new(k), new(v), new(kw),
                          pool_k, pool_v, cache_idx, T=T)
        o_at_s = jnp.sum(o_bd.reshape(DB, T, AT_HEADS, AT_KV_HEADS, AT_HEAD_DIM) * kv_eye[None, None, :, :, None], axis=3)
        o_at = jnp.concatenate([o_at_p, o_at_s.reshape(NS, AT_Q)], axis=0)

        x1, x1b, rank_t, gatew_t, cnt_t = mix_out(o_rw, o_at, gate, x, w_rwp[l], w_atp[l], w_o[l], ln1_g[l][None, :],
                                                  ln1_b[l][None, :], r_w[l], r_b[l], tm=TM, alpha=alpha)
        x = moe(x1, x1b, rank_t, gatew_t, cnt_t, layer, e1, e3, e2, s1[l], s3[l], s2[l],
                ln2_g[l][None, :], ln2_b[l][None, :], BM=BM, tm=TM, alpha=alpha)

    st = lambda name: jnp.stack(outs[name])
    return (x[:NP].reshape(B, S, d), x[NP:].reshape(DB, T, d),
            st("kp"), st("vp"), st("ip"), st("rp"), st("sp"),
            st("ks"), st("vs"), st("is"), st("rs"), st("ss"))
```
